```python
import math
import jax, jax.numpy as jnp
from jax import lax
import numpy as np

D_MODEL = 1024
BATCH = 4
SEQ = 8192
DEPTH = 4

N_MIXERS = 2
CHUNK = 128
SG_WIDTH = 2 * D_MODEL
SG_GROUPS = 8
SG_GROUP_DIM = SG_WIDTH // SG_GROUPS
RET_HEADS = 4
RET_DK = 256
RET_DV = 512
RET_QK_WIDTH = RET_HEADS * RET_DK
RET_V_WIDTH = RET_HEADS * RET_DV
RET_IN_WIDTH = 2 * RET_QK_WIDTH + 2 * RET_V_WIDTH
ROPE_BASE = 10000.0
FFN_WIDTH = 4 * D_MODEL
N_MOD = 6
N_A_LAYERS = (DEPTH + 1) // 2
N_B_LAYERS = DEPTH // 2

kernel_name = "hybrid_gmlp_retention_adaln_trunk"


def rms_norm(x, g, eps=1e-6):
    xf = x.astype(jnp.float32)
    y = xf * lax.rsqrt(jnp.mean(xf * xf, axis=-1, keepdims=True) + eps)
    return (y * g.astype(jnp.float32)).astype(x.dtype)


def layer_norm(x, g, b, eps=1e-5):
    xf = x.astype(jnp.float32)
    mu = jnp.mean(xf, axis=-1, keepdims=True)
    var = jnp.mean(jnp.square(xf - mu), axis=-1, keepdims=True)
    y = (xf - mu) * lax.rsqrt(var + eps)
    return (y * g.astype(jnp.float32) + b.astype(jnp.float32)).astype(x.dtype)


def spatial_gating_mixer(h, w_in, ln_g, ln_b, w_s, b_s, w_out):
    B, S, _ = h.shape
    nc = S // CHUNK
    z = jax.nn.gelu(h @ w_in)
    u, v = jnp.split(z, 2, axis=-1)
    v = layer_norm(v, ln_g, ln_b)
    v = v.reshape(B, nc, CHUNK, SG_GROUPS, SG_GROUP_DIM)
    w_causal = jnp.tril(w_s).astype(v.dtype)
    s = jnp.einsum('gtp,bnpgc->bntgc', w_causal, v) + b_s.T[:, :, None].astype(v.dtype)
    y = u * s.reshape(B, S, SG_WIDTH)
    return y @ w_out


def rotary(t, pos):
    half = t.shape[-1] // 2
    inv_freq = ROPE_BASE ** (-jnp.arange(half, dtype=jnp.float32) / half)
    ang = pos[:, None] * inv_freq[None, :]
    cos = jnp.cos(ang)[None, :, None, :]
    sin = jnp.sin(ang)[None, :, None, :]
    tf = t.astype(jnp.float32)
    t1, t2 = tf[..., :half], tf[..., half:]
    return jnp.concatenate([t1 * cos - t2 * sin, t1 * sin + t2 * cos], axis=-1)


def chunkwise_retention(q, k, v):
    B, S, H, DK = q.shape
    DV = v.shape[-1]
    nc = S // CHUNK

    def to_chunks(t):
        return t.reshape(B, nc, CHUNK, H, t.shape[-1]).transpose(1, 0, 3, 2, 4)

    log_gamma = jnp.log(1.0 - jnp.exp2(-5.0 - jnp.arange(H, dtype=jnp.float32)))
    idx = jnp.arange(CHUNK, dtype=jnp.float32)
    diff = idx[:, None] - idx[None, :]
    decay_intra = jnp.where(diff >= 0, jnp.exp(log_gamma[:, None, None] * jnp.maximum(diff, 0.0)), 0.0)
    decay_q = jnp.exp(log_gamma[:, None] * (idx + 1.0))
    decay_k = jnp.exp(log_gamma[:, None] * (CHUNK - 1.0 - idx))
    decay_chunk = jnp.exp(log_gamma * CHUNK)

    def step(state, qkv):
        qc, kc, vc = qkv
        scores = jnp.einsum('bhnd,bhmd->bhnm', qc, kc) * decay_intra
        out = (jnp.einsum('bhnm,bhme->bhne', scores, vc)
               + jnp.einsum('bhnd,bhde->bhne', qc, state) * decay_q[:, :, None])
        state = (state * decay_chunk[:, None, None]
                 + jnp.einsum('bhmd,bhme->bhde', kc * decay_k[:, :, None], vc))
        return state, out

    init = jnp.zeros((B, H, DK, DV), jnp.float32)
    _, out = lax.scan(step, init, (to_chunks(q), to_chunks(k), to_chunks(v)))
    return out.transpose(1, 0, 3, 2, 4).reshape(B, S, H, DV)


def retention_mixer(h, w_in, gn_g, gn_b, w_out):
    B, S, _ = h.shape
    proj = h @ w_in
    q, k, v, g = jnp.split(proj, [RET_QK_WIDTH, 2 * RET_QK_WIDTH, 2 * RET_QK_WIDTH + RET_V_WIDTH], axis=-1)
    pos = jnp.arange(S, dtype=jnp.float32)
    q = rotary(q.reshape(B, S, RET_HEADS, RET_DK), pos)
    k = rotary(k.reshape(B, S, RET_HEADS, RET_DK), pos) * (RET_DK ** -0.5)
    v = v.reshape(B, S, RET_HEADS, RET_DV).astype(jnp.float32)
    o = chunkwise_retention(q, k, v)
    mu = jnp.mean(o, axis=-1, keepdims=True)
    var = jnp.mean(jnp.square(o - mu), axis=-1, keepdims=True)
    o = ((o - mu) * lax.rsqrt(var + 1e-5)).reshape(B, S, RET_V_WIDTH)
    o = o * gn_g.astype(jnp.float32) + gn_b.astype(jnp.float32)
    o = o.astype(h.dtype) * jax.nn.silu(g)
    return o @ w_out


def squared_relu_mlp(h, w_in, w_out):
    return jnp.square(jax.nn.relu(h @ w_in)) @ w_out


def setup_inputs(seed: int = 0) -> dict:
    key = jax.random.key(seed)
    ks = jax.random.split(key, 20)
    f32 = jnp.float32
    D = D_MODEL
    nrm = lambda k, shape, s: jax.random.normal(k, shape, f32) * s
    gate_offset = jnp.concatenate([jnp.zeros((2 * D,), f32), jnp.ones((D,), f32),
                                   jnp.zeros((2 * D,), f32), jnp.ones((D,), f32)])
    return {
        "x": nrm(ks[0], (BATCH, SEQ, D), 1.0),
        "c": nrm(ks[1], (BATCH, D), 1.0),
        "ada_w": nrm(ks[2], (DEPTH, D, N_MOD * D), 0.1 * D ** -0.5),
        "ada_b": nrm(ks[3], (DEPTH, N_MOD * D), 0.02) + gate_offset,
        "pre_mix_g": 1.0 + nrm(ks[4], (DEPTH, D), 0.02),
        "post_mix_g": 1.0 + nrm(ks[5], (DEPTH, D), 0.02),
        "pre_ffn_g": 1.0 + nrm(ks[6], (DEPTH, D), 0.02),
        "post_ffn_g": 1.0 + nrm(ks[7], (DEPTH, D), 0.02),
        "ffn_w_in": nrm(ks[8], (DEPTH, D, FFN_WIDTH), D ** -0.5),
        "ffn_w_out": nrm(ks[9], (DEPTH, FFN_WIDTH, D), FFN_WIDTH ** -0.5),
        "sg_w_in": nrm(ks[10], (N_A_LAYERS, D, 2 * SG_WIDTH), D ** -0.5),
        "sg_ln_g": 1.0 + nrm(ks[11], (N_A_LAYERS, SG_WIDTH), 0.02),
        "sg_ln_b": nrm(ks[12], (N_A_LAYERS, SG_WIDTH), 0.02),
        "sg_w_s": nrm(ks[13], (N_A_LAYERS, SG_GROUPS, CHUNK, CHUNK), 0.5 * CHUNK ** -0.5),
        "sg_b_s": 1.0 + nrm(ks[14], (N_A_LAYERS, SG_GROUPS, CHUNK), 0.1),
        "sg_w_out": nrm(ks[15], (N_A_LAYERS, SG_WIDTH, D), SG_WIDTH ** -0.5),
        "ret_w_in": nrm(ks[16], (N_B_LAYERS, D, RET_IN_WIDTH), D ** -0.5),
        "ret_gn_g": 1.0 + nrm(ks[17], (N_B_LAYERS, RET_V_WIDTH), 0.02),
        "ret_gn_b": nrm(ks[18], (N_B_LAYERS, RET_V_WIDTH), 0.02),
        "ret_w_out": nrm(ks[19], (N_B_LAYERS, RET_V_WIDTH, D), RET_V_WIDTH ** -0.5),
    }


def reference(x, c, ada_w, ada_b, pre_mix_g, post_mix_g, pre_ffn_g, post_ffn_g, ffn_w_in, ffn_w_out,
              sg_w_in, sg_ln_g, sg_ln_b, sg_w_s, sg_b_s, sg_w_out,
              ret_w_in, ret_gn_g, ret_gn_b, ret_w_out):
    c_act = jax.nn.silu(c)
    for i in range(DEPTH):
        mod = c_act @ ada_w[i] + ada_b[i]
        shift_m, scale_m, gate_m, shift_f, scale_f, gate_f = [m[:, None, :] for m in jnp.split(mod, N_MOD, axis=-1)]
        h = rms_norm(x, pre_mix_g[i]) * (1.0 + scale_m) + shift_m
        j = i // N_MIXERS
        if i % N_MIXERS == 0:
            y = spatial_gating_mixer(h, sg_w_in[j], sg_ln_g[j], sg_ln_b[j], sg_w_s[j], sg_b_s[j], sg_w_out[j])
        else:
            y = retention_mixer(h, ret_w_in[j], ret_gn_g[j], ret_gn_b[j], ret_w_out[j])
        x = x + gate_m * rms_norm(y, post_mix_g[i])
        h = rms_norm(x, pre_ffn_g[i]) * (1.0 + scale_f) + shift_f
        y = squared_relu_mlp(h, ffn_w_in[i], ffn_w_out[i])
        x = x + gate_f * rms_norm(y, post_ffn_g[i])
    return x
```

```python
import functools
import math

import jax
import jax.numpy as jnp
from jax import lax
from jax.experimental import pallas as pl
from jax.experimental.pallas import tpu as pltpu

D_MODEL = 1024
CHUNK = 128
SG_WIDTH = 2 * D_MODEL
SG_GROUPS = 8
SG_GROUP_DIM = SG_WIDTH // SG_GROUPS
RET_HEADS = 4
RET_DK = 256
RET_DV = 512
RET_QK_WIDTH = RET_HEADS * RET_DK
RET_V_WIDTH = RET_HEADS * RET_DV
RET_IN_WIDTH = 2 * RET_QK_WIDTH + 2 * RET_V_WIDTH
ROPE_BASE = 10000.0
FFN_WIDTH = 4 * D_MODEL
N_MOD = 6
RMS_EPS = 1e-6
LN_EPS = 1e-5

VMEM_LIMIT_BYTES_V7X = 56 * 1024 * 1024
MOD_ROWS = 8

FFN_TILE_ROWS = 256
SG_TILE_ROWS = 256
RET_TILE_ROWS = 256

_BF16 = jnp.bfloat16
_F32 = jnp.float32


def _compiler_params():
    return pltpu.CompilerParams(
        dimension_semantics=("arbitrary",),
        vmem_limit_bytes=VMEM_LIMIT_BYTES_V7X,
    )


def _resident(shape):
    zeros = (0,) * len(shape)
    return pl.BlockSpec(shape, lambda i: zeros, pipeline_mode=pl.Buffered(1))


def _rms_scale(v):
    return lax.rsqrt(jnp.mean(v * v, axis=-1, keepdims=True) + RMS_EPS)


def _dot(a, b):
    return jnp.dot(a, b, preferred_element_type=_F32)


def _ada_kernel(c_ref, w_ref, b_ref, g_ref, o_ref):
    j = pl.program_id(1)
    c = c_ref[...]
    c_act = c * jax.nn.sigmoid(c)
    raw = jnp.dot(c_act, w_ref[...], preferred_element_type=_F32,
                  precision=lax.Precision.HIGHEST) + b_ref[...]
    is_scale = jnp.logical_or(j == 1, j == 4)
    one = jnp.where(is_scale, 1.0, 0.0).astype(_F32)
    o_ref[...] = (raw + one) * g_ref[...]


def _ada_modulation(c, ada_w, ada_b, gains):
    depth = ada_w.shape[0]
    batch = c.shape[0]
    d = D_MODEL
    c_pad = jnp.zeros((MOD_ROWS, d), _F32).at[:batch].set(c)
    b4 = ada_b.reshape(depth, N_MOD, 1, d)
    return pl.pallas_call(
        _ada_kernel,
        out_shape=jax.ShapeDtypeStruct((depth, N_MOD, MOD_ROWS, d), _F32),
        grid=(depth, N_MOD),
        in_specs=[
            pl.BlockSpec((MOD_ROWS, d), lambda i, j: (0, 0)),
            pl.BlockSpec((None, d, d), lambda i, j: (i, 0, j)),
            pl.BlockSpec((None, None, 1, d), lambda i, j: (i, j, 0, 0)),
            pl.BlockSpec((None, None, 1, d), lambda i, j: (i, j, 0, 0)),
        ],
        out_specs=pl.BlockSpec((None, None, MOD_ROWS, d), lambda i, j: (i, j, 0, 0)),
        compiler_params=pltpu.CompilerParams(
            dimension_semantics=("arbitrary", "arbitrary"),
            vmem_limit_bytes=VMEM_LIMIT_BYTES_V7X,
        ),
        name="ada_modulation",
    )(c_pad, ada_w, b4, gains)


def _mod_spec(mod_row_base, tiles_per_batch):
    return pl.BlockSpec((None, 1, D_MODEL),
                        lambda i: (mod_row_base + i // tiles_per_batch, 0, 0))


def _prenorm(x, a_ref, s_ref):
    return (x * _rms_scale(x) * a_ref[...] + s_ref[...]).astype(_BF16)


def _residual(x, y, g_ref):
    return x + y * _rms_scale(y) * g_ref[...]


def _ffn_kernel(x_ref, a_ref, s_ref, g_ref, w_in_ref, w_out_ref, o_ref):
    x = x_ref[...]
    h = _prenorm(x, a_ref, s_ref)
    a = _dot(h, w_in_ref[...])
    r = jnp.maximum(a, 0.0)
    r = (r * r).astype(_BF16)
    y = _dot(r, w_out_ref[...])
    o_ref[...] = _residual(x, y, g_ref)


def _ffn_sublayer(x, mod, layer, w_in, w_out, seq):
    rows, d = x.shape
    tm = FFN_TILE_ROWS
    tpb = seq // tm
    base = (layer * N_MOD + 3) * MOD_ROWS
    row_spec = pl.BlockSpec((tm, d), lambda i: (i, 0))
    return pl.pallas_call(
        _ffn_kernel,
        out_shape=jax.ShapeDtypeStruct((rows, d), _F32),
        grid=(rows // tm,),
        in_specs=[
            row_spec,
            _mod_spec(base + MOD_ROWS, tpb),
            _mod_spec(base, tpb),
            _mod_spec(base + 2 * MOD_ROWS, tpb),
            _resident(w_in.shape),
            _resident(w_out.shape),
        ],
        out_specs=row_spec,
        compiler_params=_compiler_params(),
        name=f"ffn_sublayer_{layer}",
    )(x, mod, mod, mod, w_in, w_out)


def _gelu_tanh(z):
    c = math.sqrt(2.0 / math.pi)
    return 0.5 * z * (1.0 + jnp.tanh(c * (z + 0.044715 * (z * z * z))))


def _sg_kernel(x_ref, a_ref, s_ref, g_ref, w_in_ref, ln_g_ref, ln_b_ref, w_s_ref, b_s_ref,
               w_out_ref, o_ref, y_ref):
    tm = x_ref.shape[0]
    x = x_ref[...]
    h = _prenorm(x, a_ref, s_ref)
    z = _gelu_tanh(_dot(h, w_in_ref[...]))
    u = z[:, :SG_WIDTH]
    v = z[:, SG_WIDTH:]
    mu = jnp.mean(v, axis=-1, keepdims=True)
    vc = v - mu
    var = jnp.mean(vc * vc, axis=-1, keepdims=True)
    vn = (vc * lax.rsqrt(var + LN_EPS) * ln_g_ref[...] + ln_b_ref[...]).astype(_BF16)

    row = lax.broadcasted_iota(jnp.int32, (CHUNK, CHUNK), 0)
    col = lax.broadcasted_iota(jnp.int32, (CHUNK, CHUNK), 1)
    causal = row >= col
    for g in range(SG_GROUPS):
        w_g = jnp.where(causal, w_s_ref[g], 0.0).astype(_BF16)
        b_g = jnp.broadcast_to(b_s_ref[:, g:g + 1], (CHUNK, SG_GROUP_DIM))
        cols = slice(g * SG_GROUP_DIM, (g + 1) * SG_GROUP_DIM)
        for n in range(tm // CHUNK):
            rows = slice(n * CHUNK, (n + 1) * CHUNK)
            s = _dot(w_g, vn[rows, cols]) + b_g
            y_ref[rows, cols] = (u[rows, cols] * s).astype(_BF16)

    y = _dot(y_ref[...], w_out_ref[...])
    o_ref[...] = _residual(x, y, g_ref)


def _sg_sublayer(x, mod, layer, w_in, ln_g, ln_b, w_s, b_s_t, w_out, seq):
    rows, d = x.shape
    tm = SG_TILE_ROWS
    tpb = seq // tm
    base = layer * N_MOD * MOD_ROWS
    row_spec = pl.BlockSpec((tm, d), lambda i: (i, 0))
    return pl.pallas_call(
        _sg_kernel,
        out_shape=jax.ShapeDtypeStruct((rows, d), _F32),
        grid=(rows // tm,),
        in_specs=[
            row_spec,
            _mod_spec(base + MOD_ROWS, tpb),
            _mod_spec(base, tpb),
            _mod_spec(base + 2 * MOD_ROWS, tpb),
            _resident(w_in.shape),
            _resident(ln_g.shape),
            _resident(ln_b.shape),
            _resident(w_s.shape),
            _resident(b_s_t.shape),
            _resident(w_out.shape),
        ],
        out_specs=row_spec,
        scratch_shapes=[pltpu.VMEM((tm, SG_WIDTH), _BF16)],
        compiler_params=_compiler_params(),
        name=f"sg_sublayer_{layer}",
    )(x, mod, mod, mod, w_in, ln_g, ln_b, w_s, b_s_t, w_out)


def _rotate(t, cos, sin):
    half = RET_DK // 2
    t1 = t[:, :half]
    t2 = t[:, half:]
    return jnp.concatenate([t1 * cos - t2 * sin, t1 * sin + t2 * cos], axis=-1)


def _ret_kernel(x_ref, a_ref, s_ref, g_ref, w_in_ref, cos_ref, sin_ref, d_intra_ref, d_q_ref,
                d_k_ref, d_c_ref, gn_g_ref, gn_b_ref, w_out_ref, o_ref, state_ref, ret_ref,
                *, tiles_per_batch):
    tm = x_ref.shape[0]

    @pl.when(pl.program_id(0) % tiles_per_batch == 0)
    def _():
        state_ref[...] = jnp.zeros_like(state_ref)

    x = x_ref[...]
    h = _prenorm(x, a_ref, s_ref)
    proj = _dot(h, w_in_ref[...])
    cos = cos_ref[...]
    sin = sin_ref[...]
    k_scale = RET_DK ** -0.5

    for hd in range(RET_HEADS):
        q = _rotate(proj[:, hd * RET_DK:(hd + 1) * RET_DK], cos, sin)
        k = _rotate(proj[:, RET_QK_WIDTH + hd * RET_DK:RET_QK_WIDTH + (hd + 1) * RET_DK],
                    cos, sin) * k_scale
        v_lo = 2 * RET_QK_WIDTH + hd * RET_DV
        v = proj[:, v_lo:v_lo + RET_DV].astype(_BF16)
        d_intra = d_intra_ref[hd]
        d_q = d_q_ref[hd]
        d_k = d_k_ref[hd]
        d_c = d_c_ref[hd]
        for n in range(tm // CHUNK):
            rows = slice(n * CHUNK, (n + 1) * CHUNK)
            qc = q[rows]
            qb = qc.astype(_BF16)
            kb = k[rows].astype(_BF16)
            vb = v[rows]
            scores = lax.dot_general(qb, kb, (((1,), (1,)), ((), ())),
                                     preferred_element_type=_F32) * d_intra
            state = state_ref[hd]
            out = (_dot(scores.astype(_BF16), vb)
                   + _dot((qc * d_q).astype(_BF16), state.astype(_BF16)))
            ret_ref[rows, hd * RET_DV:(hd + 1) * RET_DV] = out
            kd = (k[rows] * d_k).astype(_BF16)
            update = lax.dot_general(kd, vb, (((0,), (0,)), ((), ())),
                                     preferred_element_type=_F32)
            state_ref[hd] = state * d_c + update

    gate_lo = 2 * RET_QK_WIDTH + RET_V_WIDTH
    for hd in range(RET_HEADS):
        cols = slice(hd * RET_DV, (hd + 1) * RET_DV)
        o = ret_ref[:, cols]
        mu = jnp.mean(o, axis=-1, keepdims=True)
        oc = o - mu
        var = jnp.mean(oc * oc, axis=-1, keepdims=True)
        on = oc * lax.rsqrt(var + LN_EPS) * gn_g_ref[:, cols] + gn_b_ref[:, cols]
        gate = proj[:, gate_lo + hd * RET_DV:gate_lo + (hd + 1) * RET_DV]
        ret_ref[:, cols] = on * (gate * jax.nn.sigmoid(gate))

    y = _dot(ret_ref[...].astype(_BF16), w_out_ref[...])
    o_ref[...] = _residual(x, y, g_ref)


def _retention_tables(seq):
    half = RET_DK // 2
    inv_freq = ROPE_BASE ** (-jnp.arange(half, dtype=_F32) / half)
    pos = jnp.arange(seq, dtype=_F32)
    ang = pos[:, None] * inv_freq[None, :]
    cos = jnp.cos(ang)
    sin = jnp.sin(ang)
    log_gamma = jnp.log(1.0 - jnp.exp2(-5.0 - jnp.arange(RET_HEADS, dtype=_F32)))
    idx = jnp.arange(CHUNK, dtype=_F32)
    diff = idx[:, None] - idx[None, :]
    d_intra = jnp.where(diff >= 0,
                        jnp.exp(log_gamma[:, None, None] * jnp.maximum(diff, 0.0)), 0.0)
    d_q = jnp.exp(log_gamma[:, None] * (idx + 1.0))
    d_k = jnp.exp(log_gamma[:, None] * (CHUNK - 1.0 - idx))
    d_c = jnp.exp(log_gamma * CHUNK)
    d_q = jnp.broadcast_to(d_q[:, :, None], (RET_HEADS, CHUNK, RET_DK))
    d_k = jnp.broadcast_to(d_k[:, :, None], (RET_HEADS, CHUNK, RET_DK))
    d_c = jnp.broadcast_to(d_c[:, None, None], (RET_HEADS, 1, RET_DV))
    return cos, sin, d_intra, d_q, d_k, d_c


def _ret_sublayer(x, mod, layer, w_in, gn_g, gn_b, w_out, tables, seq):
    rows, d = x.shape
    tm = RET_TILE_ROWS
    tpb = seq // tm
    base = layer * N_MOD * MOD_ROWS
    cos, sin, d_intra, d_q, d_k, d_c = tables
    row_spec = pl.BlockSpec((tm, d), lambda i: (i, 0))
    pos_spec = pl.BlockSpec((tm, RET_DK // 2), lambda i: (i % tpb, 0))
    return pl.pallas_call(
        functools.partial(_ret_kernel, tiles_per_batch=tpb),
        out_shape=jax.ShapeDtypeStruct((rows, d), _F32),
        grid=(rows // tm,),
        in_specs=[
            row_spec,
            _mod_spec(base + MOD_ROWS, tpb),
            _mod_spec(base, tpb),
            _mod_spec(base + 2 * MOD_ROWS, tpb),
            _resident(w_in.shape),
            pos_spec,
            pos_spec,
            _resident(d_intra.shape),
            _resident(d_q.shape),
            _resident(d_k.shape),
            _resident(d_c.shape),
            _resident(gn_g.shape),
            _resident(gn_b.shape),
            _resident(w_out.shape),
        ],
        out_specs=row_spec,
        scratch_shapes=[
            pltpu.VMEM((RET_HEADS, RET_DK, RET_DV), _F32),
            pltpu.VMEM((tm, RET_V_WIDTH), _F32),
        ],
        compiler_params=_compiler_params(),
        name=f"ret_sublayer_{layer}",
    )(x, mod, mod, mod, w_in, cos, sin, d_intra, d_q, d_k, d_c, gn_g, gn_b, w_out)


def kernel(x, c, ada_w, ada_b, pre_mix_g, post_mix_g, pre_ffn_g, post_ffn_g, ffn_w_in, ffn_w_out,
           sg_w_in, sg_ln_g, sg_ln_b, sg_w_s, sg_b_s, sg_w_out,
           ret_w_in, ret_gn_g, ret_gn_b, ret_w_out):
    batch, seq, d = x.shape
    depth = ada_w.shape[0]
    assert d == D_MODEL and batch <= MOD_ROWS
    assert seq % max(FFN_TILE_ROWS, SG_TILE_ROWS, RET_TILE_ROWS) == 0

    ones = jnp.ones_like(pre_mix_g)
    gains = jnp.stack([ones, pre_mix_g, post_mix_g, ones, pre_ffn_g, post_ffn_g], axis=1)
    mod = _ada_modulation(c, ada_w, ada_b, gains.reshape(depth, N_MOD, 1, d))
    mod = mod.reshape(depth * N_MOD * MOD_ROWS, 1, d)

    tables = _retention_tables(seq)
    xf = x.reshape(batch * seq, d)
    for i in range(depth):
        j = i // 2
        if i % 2 == 0:
            xf = _sg_sublayer(
                xf, mod, i, sg_w_in[j].astype(_BF16), sg_ln_g[j].reshape(1, SG_WIDTH),
                sg_ln_b[j].reshape(1, SG_WIDTH), sg_w_s[j], sg_b_s[j].T,
                sg_w_out[j].astype(_BF16), seq)
        else:
            xf = _ret_sublayer(
                xf, mod, i, ret_w_in[j].astype(_BF16), ret_gn_g[j].reshape(1, RET_V_WIDTH),
                ret_gn_b[j].reshape(1, RET_V_WIDTH), ret_w_out[j].astype(_BF16), tables, seq)
        xf = _ffn_sublayer(xf, mod, i, ffn_w_in[i].astype(_BF16), ffn_w_out[i].astype(_BF16), seq)
    return xf.reshape(batch, seq, d)
```

```python
import functools
import math

import jax
import jax.numpy as jnp
from jax import lax
from jax.experimental import pallas as pl
from jax.experimental.pallas import tpu as pltpu

D_MODEL = 1024
CHUNK = 128
SG_WIDTH = 2 * D_MODEL
SG_GROUPS = 8
SG_GROUP_DIM = SG_WIDTH // SG_GROUPS
RET_HEADS = 4
RET_DK = 256
RET_DV = 512
RET_QK_WIDTH = RET_HEADS * RET_DK
RET_V_WIDTH = RET_HEADS * RET_DV
RET_IN_WIDTH = 2 * RET_QK_WIDTH + 2 * RET_V_WIDTH
ROPE_BASE = 10000.0
FFN_WIDTH = 4 * D_MODEL
N_MOD = 6
RMS_EPS = 1e-6
LN_EPS = 1e-5

VMEM_LIMIT_BYTES_V7X = 56 * 1024 * 1024
MOD_ROWS = 8

FFN_TILE_ROWS = 512
SG_TILE_ROWS = 256
RET_TILE_ROWS = 256
LANES = 128
SG_PIECE_ROWS = 16
SG_PIECE_CHAINS = 2
PRENORM_PIECE_ROWS = 16
PRENORM_CHAINS = 2

_BF16 = jnp.bfloat16
_F32 = jnp.float32


def _compiler_params(flags=None):
    return pltpu.CompilerParams(
        dimension_semantics=("arbitrary",),
        vmem_limit_bytes=VMEM_LIMIT_BYTES_V7X,
        flags=flags,
    )


def _resident(shape):
    zeros = (0,) * len(shape)
    return pl.BlockSpec(shape, lambda i: zeros, pipeline_mode=pl.Buffered(1))


def _resident_layer(stacked_shape, index):
    block_index = (index,) + (0,) * (len(stacked_shape) - 1)
    return pl.BlockSpec((None,) + tuple(stacked_shape[1:]), lambda i: block_index,
                        pipeline_mode=pl.Buffered(1))


def _rms_scale(v):
    return lax.rsqrt(jnp.mean(v * v, axis=-1, keepdims=True) + RMS_EPS)


def _dot(a, b):
    return jnp.dot(a, b, preferred_element_type=_F32)


def _ada_kernel(c_ref, w_ref, b_ref, g_ref, o_ref):
    j = pl.program_id(1)
    c = c_ref[...]
    c_act = c * jax.nn.sigmoid(c)
    raw = jnp.dot(c_act, w_ref[...], preferred_element_type=_F32,
                  precision=lax.Precision.HIGHEST) + b_ref[...]
    is_scale = jnp.logical_or(j == 1, j == 4)
    one = jnp.where(is_scale, 1.0, 0.0).astype(_F32)
    o_ref[...] = (raw + one) * g_ref[...]


def _ada_modulation(c, ada_w, ada_b, gains):
    depth = ada_w.shape[0]
    batch = c.shape[0]
    d = D_MODEL
    c_pad = jnp.zeros((MOD_ROWS, d), _F32).at[:batch].set(c)
    b4 = ada_b.reshape(depth, N_MOD, 1, d)
    return pl.pallas_call(
        _ada_kernel,
        out_shape=jax.ShapeDtypeStruct((depth, N_MOD, MOD_ROWS, d), _F32),
        grid=(depth, N_MOD),
        in_specs=[
            pl.BlockSpec((MOD_ROWS, d), lambda i, j: (0, 0)),
            pl.BlockSpec((None, d, d), lambda i, j: (i, 0, j)),
            pl.BlockSpec((None, None, 1, d), lambda i, j: (i, j, 0, 0)),
            pl.BlockSpec((None, None, 1, d), lambda i, j: (i, j, 0, 0)),
        ],
        out_specs=pl.BlockSpec((None, None, MOD_ROWS, d), lambda i, j: (i, j, 0, 0)),
        compiler_params=pltpu.CompilerParams(
            dimension_semantics=("arbitrary", "arbitrary"),
            vmem_limit_bytes=VMEM_LIMIT_BYTES_V7X,
        ),
        name="ada_modulation",
    )(c_pad, ada_w, b4, gains)


def _prenorm(x, a_ref, s_ref, chained):
    a = a_ref[...]
    shift = s_ref[...]
    pieces = []
    anchors = [None] * PRENORM_CHAINS
    for p in range(x.shape[0] // PRENORM_PIECE_ROWS):
        xp = x[p * PRENORM_PIECE_ROWS:(p + 1) * PRENORM_PIECE_ROWS]
        anchor = anchors[p % PRENORM_CHAINS]
        if chained and anchor is not None:
            xp = jnp.concatenate([xp[:, :LANES] + _derived_zero(anchor), xp[:, LANES:]], axis=1)
        h = xp * _rms_scale(xp) * a + shift
        anchors[p % PRENORM_CHAINS] = h[:, :LANES]
        pieces.append(h.astype(_BF16))
    return jnp.concatenate(pieces, axis=0)


def _residual(x, y, g_ref):
    return x + y * _rms_scale(y) * g_ref[...]


def _ffn_kernel(x_ref, a_ref, s_ref, g_ref, w_in_ref, w_out_ref, o_ref, x_even, r_even,
                x_odd, r_odd):
    s = pl.program_id(0)

    @pl.when(s == 0)
    def _():
        x_odd[...] = jnp.zeros_like(x_odd)
        r_odd[...] = jnp.zeros_like(r_odd)

    def step(x_cur, r_cur, x_prev, r_prev):
        y = _dot(r_prev[...], w_out_ref[...])
        o_ref[...] = _residual(x_prev[...], y, g_ref)
        x = x_ref[...]
        x_cur[...] = x
        a = _dot(_prenorm(x, a_ref, s_ref, chained=True), w_in_ref[...])
        r = jnp.maximum(a, 0.0)
        r_cur[...] = (r * r).astype(_BF16)

    @pl.when(s % 2 == 0)
    def _():
        step(x_even, r_even, x_odd, r_odd)

    @pl.when(s % 2 == 1)
    def _():
        step(x_odd, r_odd, x_even, r_even)


def _skewed_specs(tm, d, n_tiles, tpb, base):
    last = n_tiles - 1

    def cur(s):
        return jnp.minimum(s, last)

    def prev(s):
        return jnp.maximum(s - 1, 0)

    def mod_spec(row_base, tile):
        return pl.BlockSpec((None, 1, d), lambda s: (row_base + tile(s) // tpb, 0, 0))

    x_spec = pl.BlockSpec((tm, d), lambda s: (cur(s), 0))
    out_spec = pl.BlockSpec((tm, d), lambda s: (prev(s), 0))
    return (x_spec,
            mod_spec(base + MOD_ROWS, cur),
            mod_spec(base, cur),
            mod_spec(base + 2 * MOD_ROWS, prev),
            out_spec)


def _ffn_sublayer(x, mod, layer, w_in, w_out, seq):
    rows, d = x.shape
    tm = FFN_TILE_ROWS
    n_tiles = rows // tm
    base = (layer * N_MOD + 3) * MOD_ROWS
    x_spec, a_spec, s_spec, g_spec, out_spec = _skewed_specs(tm, d, n_tiles, seq // tm, base)
    return pl.pallas_call(
        _ffn_kernel,
        out_shape=jax.ShapeDtypeStruct((rows, d), _F32),
        grid=(n_tiles + 1,),
        in_specs=[x_spec, a_spec, s_spec, g_spec,
                  _resident_layer(w_in.shape, layer), _resident_layer(w_out.shape, layer)],
        out_specs=out_spec,
        scratch_shapes=[
            pltpu.VMEM((tm, d), _F32),
            pltpu.VMEM((tm, FFN_WIDTH), _BF16),
            pltpu.VMEM((tm, d), _F32),
            pltpu.VMEM((tm, FFN_WIDTH), _BF16),
        ],
        compiler_params=_compiler_params(),
        name=f"ffn_sublayer_{layer}",
    )(x, mod, mod, mod, w_in, w_out)


def _derived_zero(tile):
    bits = pltpu.bitcast(tile, jnp.uint32)
    sixteen = jnp.uint32(16)
    cleared = lax.shift_right_logical(lax.shift_right_logical(bits, sixteen), sixteen)
    return pltpu.bitcast(cleared, _F32)


def _lane_tile_sum(t):
    acc = t[:, :LANES]
    for i in range(1, t.shape[1] // LANES):
        acc = acc + t[:, i * LANES:(i + 1) * LANES]
    return acc


def _gelu_tanh(z, zero_tile=None):
    c = math.sqrt(2.0 / math.pi)
    coef = 0.044715
    if zero_tile is not None:
        coef = coef + jnp.tile(zero_tile, (1, z.shape[1] // LANES))
    return 0.5 * z * (1.0 + jnp.tanh(c * (z + coef * (z * z * z))))


def _sg_kernel(x_ref, a_ref, s_ref, g_ref, w_in_ref, ln_g_ref, ln_b_ref, w_s_ref, b_s_ref,
               w_out_ref, o_ref, x_even, z_even, x_odd, z_odd, u_ref, vn_ref, y_ref):
    tm = x_ref.shape[0]
    s = pl.program_id(0)

    @pl.when(s == 0)
    def _():
        x_odd[...] = jnp.zeros_like(x_odd)
        z_odd[...] = jnp.zeros_like(z_odd)

    def step(x_cur, z_cur, x_prev, z_prev):
        x = x_ref[...]
        x_cur[...] = x
        z_cur[...] = _dot(_prenorm(x, a_ref, s_ref, chained=False), w_in_ref[...])

        anchors = [None] * SG_PIECE_CHAINS
        for p in range(tm // SG_PIECE_ROWS):
            rows = slice(p * SG_PIECE_ROWS, (p + 1) * SG_PIECE_ROWS)
            anchor = anchors[p % SG_PIECE_CHAINS]
            zero = None if anchor is None else _derived_zero(anchor)
            u = _gelu_tanh(z_prev[rows, :SG_WIDTH], zero)
            v = _gelu_tanh(z_prev[rows, SG_WIDTH:], zero)
            mu = jnp.mean(v, axis=-1, keepdims=True)
            vc = v - mu
            var = jnp.mean(vc * vc, axis=-1, keepdims=True)
            vn = vc * lax.rsqrt(var + LN_EPS) * ln_g_ref[...] + ln_b_ref[...]
            u_ref[rows, :] = u
            vn_ref[rows, :] = vn.astype(_BF16)
            anchors[p % SG_PIECE_CHAINS] = _lane_tile_sum(u) + vn[:, :LANES]

        row = lax.broadcasted_iota(jnp.int32, (CHUNK, CHUNK), 0)
        col = lax.broadcasted_iota(jnp.int32, (CHUNK, CHUNK), 1)
        causal = row >= col
        for g in range(SG_GROUPS):
            w_g = jnp.where(causal, w_s_ref[g], 0.0).astype(_BF16)
            b_g = jnp.broadcast_to(b_s_ref[:, g:g + 1], (CHUNK, SG_GROUP_DIM))
            cols = slice(g * SG_GROUP_DIM, (g + 1) * SG_GROUP_DIM)
            for n in range(tm // CHUNK):
                rows = slice(n * CHUNK, (n + 1) * CHUNK)
                mixed = _dot(w_g, vn_ref[rows, cols]) + b_g
                y_ref[rows, cols] = (u_ref[rows, cols] * mixed).astype(_BF16)

        y = _dot(y_ref[...], w_out_ref[...])
        o_ref[...] = _residual(x_prev[...], y, g_ref)

    @pl.when(s % 2 == 0)
    def _():
        step(x_even, z_even, x_odd, z_odd)

    @pl.when(s % 2 == 1)
    def _():
        step(x_odd, z_odd, x_even, z_even)


def _sg_sublayer(x, mod, layer, index, w_in, ln_g, ln_b, w_s, b_s_t, w_out, seq):
    rows, d = x.shape
    tm = SG_TILE_ROWS
    n_tiles = rows // tm
    base = layer * N_MOD * MOD_ROWS
    x_spec, a_spec, s_spec, g_spec, out_spec = _skewed_specs(tm, d, n_tiles, seq // tm, base)
    return pl.pallas_call(
        _sg_kernel,
        out_shape=jax.ShapeDtypeStruct((rows, d), _F32),
        grid=(n_tiles + 1,),
        in_specs=[
            x_spec, a_spec, s_spec, g_spec,
            _resident_layer(w_in.shape, index),
            _resident_layer(ln_g.shape, index),
            _resident_layer(ln_b.shape, index),
            _resident_layer(w_s.shape, index),
            _resident_layer(b_s_t.shape, index),
            _resident_layer(w_out.shape, index),
        ],
        out_specs=out_spec,
        scratch_shapes=[
            pltpu.VMEM((tm, d), _F32),
            pltpu.VMEM((tm, 2 * SG_WIDTH), _F32),
            pltpu.VMEM((tm, d), _F32),
            pltpu.VMEM((tm, 2 * SG_WIDTH), _F32),
            pltpu.VMEM((tm, SG_WIDTH), _F32),
            pltpu.VMEM((tm, SG_WIDTH), _BF16),
            pltpu.VMEM((tm, SG_WIDTH), _BF16),
        ],
        compiler_params=_compiler_params(),
        name=f"sg_sublayer_{layer}",
    )(x, mod, mod, mod, w_in, ln_g, ln_b, w_s, b_s_t, w_out)


def _rotate(t, cos, sin):
    half = RET_DK // 2
    t1 = t[:, :half]
    t2 = t[:, half:]
    return jnp.concatenate([t1 * cos - t2 * sin, t1 * sin + t2 * cos], axis=-1)


def _ret_kernel(x_ref, a_ref, s_ref, g_ref, w_in_ref, cos_ref, sin_ref, d_intra_ref, d_q_ref,
                d_k_ref, d_c_ref, gn_g_ref, gn_b_ref, w_out_ref, o_ref, x_even, p_even, x_odd,
                p_odd, state_ref, ret_ref, *, tiles_per_batch):
    tm = x_ref.shape[0]
    s = pl.program_id(0)

    @pl.when(s == 0)
    def _():
        x_odd[...] = jnp.zeros_like(x_odd)
        p_odd[...] = jnp.zeros_like(p_odd)

    @pl.when(jnp.maximum(s - 1, 0) % tiles_per_batch == 0)
    def _():
        state_ref[...] = jnp.zeros_like(state_ref)

    def step(x_cur, p_cur, x_prev, p_prev):
        cos = cos_ref[...]
        sin = sin_ref[...]
        k_scale = RET_DK ** -0.5
        n_chunks = tm // CHUNK
        heads = range(RET_HEADS)
        q, k, v = [], [], []
        for hd in heads:
            q.append(_rotate(p_prev[:, hd * RET_DK:(hd + 1) * RET_DK], cos, sin))
            k_lo = RET_QK_WIDTH + hd * RET_DK
            k.append(_rotate(p_prev[:, k_lo:k_lo + RET_DK], cos, sin) * k_scale)
            v_lo = 2 * RET_QK_WIDTH + hd * RET_DV
            v.append(p_prev[:, v_lo:v_lo + RET_DV].astype(_BF16))

        scores, updates = {}, {}
        for n in range(n_chunks):
            rows = slice(n * CHUNK, (n + 1) * CHUNK)
            for hd in heads:
                scores[hd, n] = lax.dot_general(
                    q[hd][rows].astype(_BF16), k[hd][rows].astype(_BF16),
                    (((1,), (1,)), ((), ())), preferred_element_type=_F32)
            for hd in heads:
                kd = (k[hd][rows] * d_k_ref[hd]).astype(_BF16)
                updates[hd, n] = lax.dot_general(kd, v[hd][rows], (((0,), (0,)), ((), ())),
                                                 preferred_element_type=_F32)
        state = [state_ref[hd] for hd in heads]
        for n in range(n_chunks):
            rows = slice(n * CHUNK, (n + 1) * CHUNK)
            for hd in heads:
                masked = (scores[hd, n] * d_intra_ref[hd]).astype(_BF16)
                qd = (q[hd][rows] * d_q_ref[hd]).astype(_BF16)
                out = _dot(masked, v[hd][rows]) + _dot(qd, state[hd].astype(_BF16))
                ret_ref[rows, hd * RET_DV:(hd + 1) * RET_DV] = out
                state[hd] = state[hd] * d_c_ref[hd] + updates[hd, n]
        for hd in heads:
            state_ref[hd] = state[hd]

        gate_lo = 2 * RET_QK_WIDTH + RET_V_WIDTH
        for hd in range(RET_HEADS):
            cols = slice(hd * RET_DV, (hd + 1) * RET_DV)
            o = ret_ref[:, cols]
            mu = jnp.mean(o, axis=-1, keepdims=True)
            oc = o - mu
            var = jnp.mean(oc * oc, axis=-1, keepdims=True)
            on = oc * lax.rsqrt(var + LN_EPS) * gn_g_ref[:, cols] + gn_b_ref[:, cols]
            gate = p_prev[:, gate_lo + hd * RET_DV:gate_lo + (hd + 1) * RET_DV]
            ret_ref[:, cols] = on * (gate * jax.nn.sigmoid(gate))

        y = _dot(ret_ref[...].astype(_BF16), w_out_ref[...])
        o_ref[...] = _residual(x_prev[...], y, g_ref)

        x = x_ref[...]
        x_cur[...] = x
        p_cur[...] = _dot(_prenorm(x, a_ref, s_ref, chained=True), w_in_ref[...])

    @pl.when(s % 2 == 0)
    def _():
        step(x_even, p_even, x_odd, p_odd)

    @pl.when(s % 2 == 1)
    def _():
        step(x_odd, p_odd, x_even, p_even)


def _retention_tables(seq):
    half = RET_DK // 2
    inv_freq = ROPE_BASE ** (-jnp.arange(half, dtype=_F32) / half)
    pos = jnp.arange(seq, dtype=_F32)
    ang = pos[:, None] * inv_freq[None, :]
    cos = jnp.cos(ang)
    sin = jnp.sin(ang)
    log_gamma = jnp.log(1.0 - jnp.exp2(-5.0 - jnp.arange(RET_HEADS, dtype=_F32)))
    idx = jnp.arange(CHUNK, dtype=_F32)
    diff = idx[:, None] - idx[None, :]
    d_intra = jnp.where(diff >= 0,
                        jnp.exp(log_gamma[:, None, None] * jnp.maximum(diff, 0.0)), 0.0)
    d_q = jnp.exp(log_gamma[:, None] * (idx + 1.0))
    d_k = jnp.exp(log_gamma[:, None] * (CHUNK - 1.0 - idx))
    d_c = jnp.exp(log_gamma * CHUNK)
    d_q = jnp.broadcast_to(d_q[:, :, None], (RET_HEADS, CHUNK, RET_DK))
    d_k = jnp.broadcast_to(d_k[:, :, None], (RET_HEADS, CHUNK, RET_DK))
    d_c = jnp.broadcast_to(d_c[:, None, None], (RET_HEADS, 1, RET_DV))
    return cos, sin, d_intra, d_q, d_k, d_c


def _ret_sublayer(x, mod, layer, index, w_in, gn_g, gn_b, w_out, tables, seq):
    rows, d = x.shape
    tm = RET_TILE_ROWS
    n_tiles = rows // tm
    tpb = seq // tm
    base = layer * N_MOD * MOD_ROWS
    cos, sin, d_intra, d_q, d_k, d_c = tables
    x_spec, a_spec, s_spec, g_spec, out_spec = _skewed_specs(tm, d, n_tiles, tpb, base)
    pos_spec = pl.BlockSpec((tm, RET_DK // 2), lambda s: (jnp.maximum(s - 1, 0) % tpb, 0))
    return pl.pallas_call(
        functools.partial(_ret_kernel, tiles_per_batch=tpb),
        out_shape=jax.ShapeDtypeStruct((rows, d), _F32),
        grid=(n_tiles + 1,),
        in_specs=[
            x_spec, a_spec, s_spec, g_spec,
            _resident_layer(w_in.shape, index),
            pos_spec,
            pos_spec,
            _resident(d_intra.shape),
            _resident(d_q.shape),
            _resident(d_k.shape),
            _resident(d_c.shape),
            _resident_layer(gn_g.shape, index),
            _resident_layer(gn_b.shape, index),
            _resident_layer(w_out.shape, index),
        ],
        out_specs=out_spec,
        scratch_shapes=[
            pltpu.VMEM((tm, d), _F32),
            pltpu.VMEM((tm, RET_IN_WIDTH), _F32),
            pltpu.VMEM((tm, d), _F32),
            pltpu.VMEM((tm, RET_IN_WIDTH), _F32),
            pltpu.VMEM((RET_HEADS, RET_DK, RET_DV), _F32),
            pltpu.VMEM((tm, RET_V_WIDTH), _F32),
        ],
        compiler_params=_compiler_params(),
        name=f"ret_sublayer_{layer}",
    )(x, mod, mod, mod, w_in, cos, sin, d_intra, d_q, d_k, d_c, gn_g, gn_b, w_out)


def kernel(x, c, ada_w, ada_b, pre_mix_g, post_mix_g, pre_ffn_g, post_ffn_g, ffn_w_in, ffn_w_out,
           sg_w_in, sg_ln_g, sg_ln_b, sg_w_s, sg_b_s, sg_w_out,
           ret_w_in, ret_gn_g, ret_gn_b, ret_w_out):
    batch, seq, d = x.shape
    depth = ada_w.shape[0]
    assert d == D_MODEL and batch <= MOD_ROWS
    assert seq % max(FFN_TILE_ROWS, SG_TILE_ROWS, RET_TILE_ROWS) == 0

    ones = jnp.ones_like(pre_mix_g)
    gains = jnp.stack([ones, pre_mix_g, post_mix_g, ones, pre_ffn_g, post_ffn_g], axis=1)
    mod = _ada_modulation(c, ada_w, ada_b, gains.reshape(depth, N_MOD, 1, d))
    mod = mod.reshape(depth * N_MOD * MOD_ROWS, 1, d)

    ffn_w = (ffn_w_in.astype(_BF16), ffn_w_out.astype(_BF16))
    sg_params = (sg_w_in.astype(_BF16), sg_ln_g[:, None, :], sg_ln_b[:, None, :], sg_w_s,
                 jnp.swapaxes(sg_b_s, 1, 2), sg_w_out.astype(_BF16))
    ret_params = (ret_w_in.astype(_BF16), ret_gn_g[:, None, :], ret_gn_b[:, None, :],
                  ret_w_out.astype(_BF16))

    tables = _retention_tables(seq)
    xf = x.reshape(batch * seq, d)
    for i in range(depth):
        j = i // 2
        if i % 2 == 0:
            xf = _sg_sublayer(xf, mod, i, j, *sg_params, seq)
        else:
            xf = _ret_sublayer(xf, mod, i, j, *ret_params, tables, seq)
        xf = _ffn_sublayer(xf, mod, i, *ffn_w, seq)
    return xf.reshape(batch, seq, d)
```

```python
import functools
import math

import jax
import jax.numpy as jnp
from jax import lax
from jax.experimental import pallas as pl
from jax.experimental.pallas import tpu as pltpu

D_MODEL = 1024
CHUNK = 128
SG_WIDTH = 2 * D_MODEL
SG_GROUPS = 8
SG_GROUP_DIM = SG_WIDTH // SG_GROUPS
RET_HEADS = 4
RET_DK = 256
RET_DV = 512
RET_QK_WIDTH = RET_HEADS * RET_DK
RET_V_WIDTH = RET_HEADS * RET_DV
RET_IN_WIDTH = 2 * RET_QK_WIDTH + 2 * RET_V_WIDTH
ROPE_BASE = 10000.0
FFN_WIDTH = 4 * D_MODEL
N_MOD = 6
RMS_EPS = 1e-6
LN_EPS = 1e-5

VMEM_LIMIT_BYTES_V7X = 56 * 1024 * 1024
MOD_ROWS = 8

FFN_TILE_ROWS = 512
SG_TILE_ROWS = 512
RET_TILE_ROWS = 512
RET_CHUNK_WINDOW = 2
LANES = 128
SG_PIECE_ROWS = 16
PRENORM_PIECE_ROWS = 16
PRENORM_CHAINS = 2

_BF16 = jnp.bfloat16
_F32 = jnp.float32


def _compiler_params(flags=None):
    return pltpu.CompilerParams(
        dimension_semantics=("arbitrary",),
        vmem_limit_bytes=VMEM_LIMIT_BYTES_V7X,
        flags=flags,
    )


def _resident(shape):
    zeros = (0,) * len(shape)
    return pl.BlockSpec(shape, lambda i: zeros, pipeline_mode=pl.Buffered(1))


def _resident_layer(stacked_shape, index):
    block_index = (index,) + (0,) * (len(stacked_shape) - 1)
    return pl.BlockSpec((None,) + tuple(stacked_shape[1:]), lambda i: block_index,
                        pipeline_mode=pl.Buffered(1))


def _rms_scale(v):
    return lax.rsqrt(jnp.mean(v * v, axis=-1, keepdims=True) + RMS_EPS)


def _dot(a, b):
    return jnp.dot(a, b, preferred_element_type=_F32)


def _ada_kernel(c_ref, w_ref, b_ref, g_ref, o_ref):
    d = c_ref.shape[1]
    c = c_ref[...]
    c_act = c * jax.nn.sigmoid(c)
    raw = _dot(c_act.astype(_BF16), w_ref[...].astype(_BF16))
    for k, one in enumerate((0.0, 1.0, 0.0)):
        o_ref[k] = (raw[:, k * d:(k + 1) * d] + b_ref[k] + one) * g_ref[k]


def _ada_modulation(c, ada_w, ada_b, gains):
    depth = ada_w.shape[0]
    batch = c.shape[0]
    d = D_MODEL
    per_step = N_MOD // 2
    c_pad = jnp.zeros((MOD_ROWS, d), _F32).at[:batch].set(c)
    b4 = ada_b.reshape(depth, N_MOD, 1, d)
    return pl.pallas_call(
        _ada_kernel,
        out_shape=jax.ShapeDtypeStruct((depth, N_MOD, MOD_ROWS, d), _F32),
        grid=(depth, 2),
        in_specs=[
            pl.BlockSpec((MOD_ROWS, d), lambda i, j: (0, 0)),
            pl.BlockSpec((None, d, per_step * d), lambda i, j: (i, 0, j)),
            pl.BlockSpec((None, per_step, 1, d), lambda i, j: (i, j, 0, 0)),
            pl.BlockSpec((None, per_step, 1, d), lambda i, j: (i, j, 0, 0)),
        ],
        out_specs=pl.BlockSpec((None, per_step, MOD_ROWS, d), lambda i, j: (i, j, 0, 0)),
        compiler_params=pltpu.CompilerParams(
            dimension_semantics=("arbitrary", "arbitrary"),
            vmem_limit_bytes=VMEM_LIMIT_BYTES_V7X,
        ),
        name="ada_modulation",
    )(c_pad, ada_w, b4, gains)


def _derived_zero(tile):
    bits = pltpu.bitcast(tile, jnp.uint32)
    sixteen = jnp.uint32(16)
    cleared = lax.shift_right_logical(lax.shift_right_logical(bits, sixteen), sixteen)
    return pltpu.bitcast(cleared, _F32)


def _prenorm(x, a_ref, s_ref, chained):
    a = a_ref[...]
    shift = s_ref[...]
    pieces = []
    anchors = [None] * PRENORM_CHAINS
    for p in range(x.shape[0] // PRENORM_PIECE_ROWS):
        xp = x[p * PRENORM_PIECE_ROWS:(p + 1) * PRENORM_PIECE_ROWS]
        anchor = anchors[p % PRENORM_CHAINS]
        if chained and anchor is not None:
            xp = jnp.concatenate([xp[:, :LANES] + _derived_zero(anchor), xp[:, LANES:]], axis=1)
        h = xp * _rms_scale(xp) * a + shift
        anchors[p % PRENORM_CHAINS] = h[:, :LANES]
        pieces.append(h.astype(_BF16))
    return jnp.concatenate(pieces, axis=0)


def _residual(x, y, g_ref):
    return x + y * _rms_scale(y) * g_ref[...]


def _ffn_kernel(x_ref, x_prev_ref, a_ref, s_ref, g_ref, w_in_ref, w_out_ref, o_ref, r_ref):
    @pl.when(pl.program_id(0) == 0)
    def _():
        r_ref[...] = jnp.zeros_like(r_ref)

    y = _dot(r_ref[...], w_out_ref[...])
    o_ref[...] = _residual(x_prev_ref[...], y, g_ref)
    a = _dot(_prenorm(x_ref[...], a_ref, s_ref, chained=True), w_in_ref[...])
    r = jnp.maximum(a, 0.0)
    r_ref[...] = (r * r).astype(_BF16)


def _skewed_specs(tm, d, n_tiles, tpb, base, lag=1):
    last = n_tiles - 1

    def cur(s):
        return jnp.minimum(s, last)

    def prev(s):
        return jnp.maximum(s - lag, 0)

    def mod_spec(row_base, tile):
        return pl.BlockSpec((None, 1, d), lambda s: (row_base + tile(s) // tpb, 0, 0))

    x_spec = pl.BlockSpec((tm, d), lambda s: (cur(s), 0))
    out_spec = pl.BlockSpec((tm, d), lambda s: (prev(s), 0))
    return (x_spec,
            mod_spec(base + MOD_ROWS, cur),
            mod_spec(base, cur),
            mod_spec(base + 2 * MOD_ROWS, prev),
            out_spec)


def _ffn_sublayer(x, mod, layer, w_in, w_out, seq):
    rows, d = x.shape
    tm = FFN_TILE_ROWS
    n_tiles = rows // tm
    base = (layer * N_MOD + 3) * MOD_ROWS
    x_spec, a_spec, s_spec, g_spec, out_spec = _skewed_specs(tm, d, n_tiles, seq // tm, base)
    return pl.pallas_call(
        _ffn_kernel,
        out_shape=jax.ShapeDtypeStruct((rows, d), _F32),
        grid=(n_tiles + 1,),
        in_specs=[x_spec, out_spec, a_spec, s_spec, g_spec,
                  _resident_layer(w_in.shape, layer), _resident_layer(w_out.shape, layer)],
        out_specs=out_spec,
        scratch_shapes=[pltpu.VMEM((tm, FFN_WIDTH), _BF16)],
        compiler_params=_compiler_params(),
        name=f"ffn_sublayer_{layer}",
    )(x, x, mod, mod, mod, w_in, w_out)


def _gelu_tanh(z):
    c = math.sqrt(2.0 / math.pi)
    return 0.5 * z * (1.0 + jnp.tanh(c * (z + 0.044715 * (z * z * z))))


def _sg_kernel(x_ref, a_ref, s_ref, g_ref, w_in_ref, ln_g_ref, ln_b_ref, w_s_ref,
               b_s_ref, w_out_ref, o_ref, z_ref, u_ref, vn_ref, y_ref):
    tm = x_ref.shape[0]
    x = x_ref[...]
    z_ref[...] = _dot(_prenorm(x, a_ref, s_ref, chained=False), w_in_ref[...])

    for p in range(tm // SG_PIECE_ROWS):
        rows = slice(p * SG_PIECE_ROWS, (p + 1) * SG_PIECE_ROWS)
        u_ref[rows, :] = _gelu_tanh(z_ref[rows, :SG_WIDTH])
        v = _gelu_tanh(z_ref[rows, SG_WIDTH:])
        mu = jnp.mean(v, axis=-1, keepdims=True)
        vc = v - mu
        var = jnp.mean(vc * vc, axis=-1, keepdims=True)
        vn = vc * lax.rsqrt(var + LN_EPS) * ln_g_ref[...] + ln_b_ref[...]
        vn_ref[rows, :] = vn.astype(_BF16)

    row = lax.broadcasted_iota(jnp.int32, (CHUNK, CHUNK), 0)
    col = lax.broadcasted_iota(jnp.int32, (CHUNK, CHUNK), 1)
    causal = row >= col
    for g in range(SG_GROUPS):
        w_g = jnp.where(causal, w_s_ref[g], 0.0).astype(_BF16)
        b_g = jnp.broadcast_to(b_s_ref[:, g:g + 1], (CHUNK, SG_GROUP_DIM))
        cols = slice(g * SG_GROUP_DIM, (g + 1) * SG_GROUP_DIM)
        for n in range(tm // CHUNK):
            rows = slice(n * CHUNK, (n + 1) * CHUNK)
            mixed = _dot(w_g, vn_ref[rows, cols]) + b_g
            y_ref[rows, cols] = (u_ref[rows, cols] * mixed).astype(_BF16)

    y = _dot(y_ref[...], w_out_ref[...])
    o_ref[...] = _residual(x, y, g_ref)


def _sg_sublayer(x, mod, layer, index, w_in, ln_g, ln_b, w_s, b_s_t, w_out, seq):
    rows, d = x.shape
    tm = SG_TILE_ROWS
    n_tiles = rows // tm
    base = layer * N_MOD * MOD_ROWS
    x_spec, a_spec, s_spec, g_spec, out_spec = _skewed_specs(tm, d, n_tiles, seq // tm, base,
                                                             lag=0)
    return pl.pallas_call(
        _sg_kernel,
        out_shape=jax.ShapeDtypeStruct((rows, d), _F32),
        grid=(n_tiles,),
        in_specs=[
            x_spec, a_spec, s_spec, g_spec,
            _resident_layer(w_in.shape, index),
            _resident_layer(ln_g.shape, index),
            _resident_layer(ln_b.shape, index),
            _resident_layer(w_s.shape, index),
            _resident_layer(b_s_t.shape, index),
            _resident_layer(w_out.shape, index),
        ],
        out_specs=out_spec,
        scratch_shapes=[
            pltpu.VMEM((tm, 2 * SG_WIDTH), _F32),
            pltpu.VMEM((tm, SG_WIDTH), _F32),
            pltpu.VMEM((tm, SG_WIDTH), _BF16),
            pltpu.VMEM((tm, SG_WIDTH), _BF16),
        ],
        compiler_params=_compiler_params(),
        name=f"sg_sublayer_{layer}",
    )(x, mod, mod, mod, w_in, ln_g, ln_b, w_s, b_s_t, w_out)


def _rotate(t, cos, sin):
    half = RET_DK // 2
    t1 = t[:, :half]
    t2 = t[:, half:]
    return jnp.concatenate([t1 * cos - t2 * sin, t1 * sin + t2 * cos], axis=-1)


def _ret_kernel(x_ref, x_prev_ref, a_ref, s_ref, g_ref, w_in_ref, cos_ref, sin_ref, d_intra_ref,
                d_q_ref, d_k_ref, d_c_ref, gn_g_ref, gn_b_ref, w_out_ref, o_ref, p_ref, state_ref,
                ret_ref, *, tiles_per_batch):
    tm = x_ref.shape[0]
    s = pl.program_id(0)

    @pl.when(s == 0)
    def _():
        p_ref[...] = jnp.zeros_like(p_ref)

    @pl.when(jnp.maximum(s - 1, 0) % tiles_per_batch == 0)
    def _():
        state_ref[...] = jnp.zeros_like(state_ref)

    cos = cos_ref[...]
    sin = sin_ref[...]
    k_scale = RET_DK ** -0.5
    heads = range(RET_HEADS)
    q, k, v = [], [], []
    for hd in heads:
        q.append(_rotate(p_ref[:, hd * RET_DK:(hd + 1) * RET_DK], cos, sin))
        k_lo = RET_QK_WIDTH + hd * RET_DK
        k.append(_rotate(p_ref[:, k_lo:k_lo + RET_DK], cos, sin) * k_scale)
        v_lo = 2 * RET_QK_WIDTH + hd * RET_DV
        v.append(p_ref[:, v_lo:v_lo + RET_DV].astype(_BF16))

    state = [state_ref[hd] for hd in heads]
    for first in range(0, tm // CHUNK, RET_CHUNK_WINDOW):
        window = range(first, first + RET_CHUNK_WINDOW)
        scores, updates = {}, {}
        for n in window:
            rows = slice(n * CHUNK, (n + 1) * CHUNK)
            for hd in heads:
                scores[hd, n] = lax.dot_general(
                    q[hd][rows].astype(_BF16), k[hd][rows].astype(_BF16),
                    (((1,), (1,)), ((), ())), preferred_element_type=_F32)
            for hd in heads:
                kd = (k[hd][rows] * d_k_ref[hd]).astype(_BF16)
                updates[hd, n] = lax.dot_general(kd, v[hd][rows], (((0,), (0,)), ((), ())),
                                                 preferred_element_type=_F32)
        for n in window:
            rows = slice(n * CHUNK, (n + 1) * CHUNK)
            for hd in heads:
                masked = (scores[hd, n] * d_intra_ref[hd]).astype(_BF16)
                qd = (q[hd][rows] * d_q_ref[hd]).astype(_BF16)
                out = _dot(masked, v[hd][rows]) + _dot(qd, state[hd].astype(_BF16))
                ret_ref[rows, hd * RET_DV:(hd + 1) * RET_DV] = out
                state[hd] = state[hd] * d_c_ref[hd] + updates[hd, n]
    for hd in heads:
        state_ref[hd] = state[hd]

    gate_lo = 2 * RET_QK_WIDTH + RET_V_WIDTH
    for hd in heads:
        cols = slice(hd * RET_DV, (hd + 1) * RET_DV)
        o = ret_ref[:, cols]
        mu = jnp.mean(o, axis=-1, keepdims=True)
        oc = o - mu
        var = jnp.mean(oc * oc, axis=-1, keepdims=True)
        on = oc * lax.rsqrt(var + LN_EPS) * gn_g_ref[:, cols] + gn_b_ref[:, cols]
        gate = p_ref[:, gate_lo + hd * RET_DV:gate_lo + (hd + 1) * RET_DV]
        ret_ref[:, cols] = on * (gate * jax.nn.sigmoid(gate))

    y = _dot(ret_ref[...].astype(_BF16), w_out_ref[...])
    o_ref[...] = _residual(x_prev_ref[...], y, g_ref)

    p_ref[...] = _dot(_prenorm(x_ref[...], a_ref, s_ref, chained=True), w_in_ref[...])


def _retention_tables(seq):
    half = RET_DK // 2
    inv_freq = ROPE_BASE ** (-jnp.arange(half, dtype=_F32) / half)
    pos = jnp.arange(seq, dtype=_F32)
    ang = pos[:, None] * inv_freq[None, :]
    cos = jnp.cos(ang)
    sin = jnp.sin(ang)
    log_gamma = jnp.log(1.0 - jnp.exp2(-5.0 - jnp.arange(RET_HEADS, dtype=_F32)))
    idx = jnp.arange(CHUNK, dtype=_F32)
    diff = idx[:, None] - idx[None, :]
    d_intra = jnp.where(diff >= 0,
                        jnp.exp(log_gamma[:, None, None] * jnp.maximum(diff, 0.0)), 0.0)
    d_q = jnp.exp(log_gamma[:, None] * (idx + 1.0))
    d_k = jnp.exp(log_gamma[:, None] * (CHUNK - 1.0 - idx))
    d_c = jnp.exp(log_gamma * CHUNK)
    d_q = jnp.broadcast_to(d_q[:, :, None], (RET_HEADS, CHUNK, RET_DK))
    d_k = jnp.broadcast_to(d_k[:, :, None], (RET_HEADS, CHUNK, RET_DK))
    d_c = jnp.broadcast_to(d_c[:, None, None], (RET_HEADS, 1, RET_DV))
    return cos, sin, d_intra, d_q, d_k, d_c


def _ret_sublayer(x, mod, layer, index, w_in, gn_g, gn_b, w_out, tables, seq):
    rows, d = x.shape
    tm = RET_TILE_ROWS
    n_tiles = rows // tm
    tpb = seq // tm
    base = layer * N_MOD * MOD_ROWS
    cos, sin, d_intra, d_q, d_k, d_c = tables
    x_spec, a_spec, s_spec, g_spec, out_spec = _skewed_specs(tm, d, n_tiles, tpb, base)
    pos_spec = pl.BlockSpec((tm, RET_DK // 2), lambda s: (jnp.maximum(s - 1, 0) % tpb, 0))
    return pl.pallas_call(
        functools.partial(_ret_kernel, tiles_per_batch=tpb),
        out_shape=jax.ShapeDtypeStruct((rows, d), _F32),
        grid=(n_tiles + 1,),
        in_specs=[
            x_spec, out_spec, a_spec, s_spec, g_spec,
            _resident_layer(w_in.shape, index),
            pos_spec,
            pos_spec,
            _resident(d_intra.shape),
            _resident(d_q.shape),
            _resident(d_k.shape),
            _resident(d_c.shape),
            _resident_layer(gn_g.shape, index),
            _resident_layer(gn_b.shape, index),
            _resident_layer(w_out.shape, index),
        ],
        out_specs=out_spec,
        scratch_shapes=[
            pltpu.VMEM((tm, RET_IN_WIDTH), _F32),
            pltpu.VMEM((RET_HEADS, RET_DK, RET_DV), _F32),
            pltpu.VMEM((tm, RET_V_WIDTH), _F32),
        ],
        compiler_params=_compiler_params(),
        name=f"ret_sublayer_{layer}",
    )(x, x, mod, mod, mod, w_in, cos, sin, d_intra, d_q, d_k, d_c, gn_g, gn_b, w_out)


def kernel(x, c, ada_w, ada_b, pre_mix_g, post_mix_g, pre_ffn_g, post_ffn_g, ffn_w_in, ffn_w_out,
           sg_w_in, sg_ln_g, sg_ln_b, sg_w_s, sg_b_s, sg_w_out,
           ret_w_in, ret_gn_g, ret_gn_b, ret_w_out):
    batch, seq, d = x.shape
    depth = ada_w.shape[0]
    assert d == D_MODEL and batch <= MOD_ROWS
    assert seq % max(FFN_TILE_ROWS, SG_TILE_ROWS, RET_TILE_ROWS) == 0

    ones = jnp.ones_like(pre_mix_g)
    gains = jnp.stack([ones, pre_mix_g, post_mix_g, ones, pre_ffn_g, post_ffn_g], axis=1)
    mod = _ada_modulation(c, ada_w, ada_b, gains.reshape(depth, N_MOD, 1, d))
    mod = mod.reshape(depth * N_MOD * MOD_ROWS, 1, d)

    ffn_w = (ffn_w_in.astype(_BF16), ffn_w_out.astype(_BF16))
    sg_params = (sg_w_in.astype(_BF16), sg_ln_g[:, None, :], sg_ln_b[:, None, :], sg_w_s,
                 jnp.swapaxes(sg_b_s, 1, 2), sg_w_out.astype(_BF16))
    ret_params = (ret_w_in.astype(_BF16), ret_gn_g[:, None, :], ret_gn_b[:, None, :],
                  ret_w_out.astype(_BF16))

    tables = _retention_tables(seq)
    xf = x.reshape(batch * seq, d)
    for i in range(depth):
        j = i // 2
        if i % 2 == 0:
            xf = _sg_sublayer(xf, mod, i, j, *sg_params, seq)
        else:
            xf = _ret_sublayer(xf, mod, i, j, *ret_params, tables, seq)
        xf = _ffn_sublayer(xf, mod, i, *ffn_w, seq)
    return xf.reshape(batch, seq, d)
```

```python
import functools
import math

import jax
import jax.numpy as jnp
from jax import lax
from jax.experimental import pallas as pl
from jax.experimental.pallas import tpu as pltpu

D_MODEL = 1024
CHUNK = 128
SG_WIDTH = 2 * D_MODEL
SG_GROUPS = 8
SG_GROUP_DIM = SG_WIDTH // SG_GROUPS
RET_HEADS = 4
RET_DK = 256
RET_DV = 512
RET_QK_WIDTH = RET_HEADS * RET_DK
RET_V_WIDTH = RET_HEADS * RET_DV
RET_IN_WIDTH = 2 * RET_QK_WIDTH + 2 * RET_V_WIDTH
ROPE_BASE = 10000.0
FFN_WIDTH = 4 * D_MODEL
N_MOD = 6
RMS_EPS = 1e-6
LN_EPS = 1e-5

VMEM_LIMIT_BYTES_V7X = 56 * 1024 * 1024
MOD_ROWS = 8

FFN_TILE_ROWS = 512
SG_TILE_ROWS = 512
RET_TILE_ROWS = 512
RET_CHUNK_WINDOW = 2
LANES = 128
SG_PIECE_ROWS = 16
PRENORM_PIECE_ROWS = 16
PRENORM_CHAINS = 2

_BF16 = jnp.bfloat16
_F32 = jnp.float32


def _compiler_params(flags=None):
    return pltpu.CompilerParams(
        dimension_semantics=("arbitrary",),
        vmem_limit_bytes=VMEM_LIMIT_BYTES_V7X,
        flags=flags,
    )


def _resident(shape):
    zeros = (0,) * len(shape)
    return pl.BlockSpec(shape, lambda i: zeros, pipeline_mode=pl.Buffered(1))


def _resident_layer(stacked_shape, index):
    block_index = (index,) + (0,) * (len(stacked_shape) - 1)
    return pl.BlockSpec((None,) + tuple(stacked_shape[1:]), lambda i: block_index,
                        pipeline_mode=pl.Buffered(1))


def _rms_scale(v):
    return lax.rsqrt(jnp.mean(v * v, axis=-1, keepdims=True) + RMS_EPS)


def _dot(a, b):
    return jnp.dot(a, b, preferred_element_type=_F32)


def _ada_kernel(c_ref, w_ref, b_ref, g_ref, o_ref):
    d = c_ref.shape[1]
    c = c_ref[...]
    c_act = c * jax.nn.sigmoid(c)
    raw = _dot(c_act.astype(_BF16), w_ref[...].astype(_BF16))
    for k, one in enumerate((0.0, 1.0, 0.0)):
        o_ref[k] = (raw[:, k * d:(k + 1) * d] + b_ref[k] + one) * g_ref[k]


def _ada_modulation(c, ada_w, ada_b, gains):
    depth = ada_w.shape[0]
    batch = c.shape[0]
    d = D_MODEL
    per_step = N_MOD // 2
    c_pad = jnp.zeros((MOD_ROWS, d), _F32).at[:batch].set(c)
    b4 = ada_b.reshape(depth, N_MOD, 1, d)
    return pl.pallas_call(
        _ada_kernel,
        out_shape=jax.ShapeDtypeStruct((depth, N_MOD, MOD_ROWS, d), _F32),
        grid=(depth, 2),
        in_specs=[
            pl.BlockSpec((MOD_ROWS, d), lambda i, j: (0, 0)),
            pl.BlockSpec((None, d, per_step * d), lambda i, j: (i, 0, j)),
            pl.BlockSpec((None, per_step, 1, d), lambda i, j: (i, j, 0, 0)),
            pl.BlockSpec((None, per_step, 1, d), lambda i, j: (i, j, 0, 0)),
        ],
        out_specs=pl.BlockSpec((None, per_step, MOD_ROWS, d), lambda i, j: (i, j, 0, 0)),
        compiler_params=pltpu.CompilerParams(
            dimension_semantics=("arbitrary", "arbitrary"),
            vmem_limit_bytes=VMEM_LIMIT_BYTES_V7X,
        ),
        name="ada_modulation",
    )(c_pad, ada_w, b4, gains)


def _derived_zero(tile):
    bits = pltpu.bitcast(tile, jnp.uint32)
    sixteen = jnp.uint32(16)
    cleared = lax.shift_right_logical(lax.shift_right_logical(bits, sixteen), sixteen)
    return pltpu.bitcast(cleared, _F32)


def _prenorm(x, a_ref, s_ref, chained):
    a = a_ref[...]
    shift = s_ref[...]
    pieces = []
    anchors = [None] * PRENORM_CHAINS
    for p in range(x.shape[0] // PRENORM_PIECE_ROWS):
        xp = x[p * PRENORM_PIECE_ROWS:(p + 1) * PRENORM_PIECE_ROWS]
        anchor = anchors[p % PRENORM_CHAINS]
        if chained and anchor is not None:
            xp = jnp.concatenate([xp[:, :LANES] + _derived_zero(anchor), xp[:, LANES:]], axis=1)
        h = xp * _rms_scale(xp) * a + shift
        anchors[p % PRENORM_CHAINS] = h[:, :LANES]
        pieces.append(h.astype(_BF16))
    return jnp.concatenate(pieces, axis=0)


def _residual(x, y, g_ref):
    return x + y * _rms_scale(y) * g_ref[...]


def _ffn_kernel(x_ref, x_prev_ref, a_ref, s_ref, g_ref, w_in_ref, w_out_ref, o_ref, r_ref):
    @pl.when(pl.program_id(0) == 0)
    def _():
        r_ref[...] = jnp.zeros_like(r_ref)

    y = _dot(r_ref[...], w_out_ref[...])
    o_ref[...] = _residual(x_prev_ref[...], y, g_ref)
    a = _dot(_prenorm(x_ref[...], a_ref, s_ref, chained=True), w_in_ref[...])
    r = jnp.maximum(a, 0.0)
    r_ref[...] = (r * r).astype(_BF16)


def _skewed_specs(tm, d, n_tiles, tpb, base, lag=1):
    last = n_tiles - 1

    def cur(s):
        return jnp.minimum(s, last)

    def prev(s):
        return jnp.maximum(s - lag, 0)

    def mod_spec(row_base, tile):
        return pl.BlockSpec((None, 1, d), lambda s: (row_base + tile(s) // tpb, 0, 0))

    x_spec = pl.BlockSpec((tm, d), lambda s: (cur(s), 0))
    out_spec = pl.BlockSpec((tm, d), lambda s: (prev(s), 0))
    return (x_spec,
            mod_spec(base + MOD_ROWS, cur),
            mod_spec(base, cur),
            mod_spec(base + 2 * MOD_ROWS, prev),
            out_spec)


def _ffn_sublayer(x, mod, layer, w_in, w_out, seq):
    rows, d = x.shape
    tm = FFN_TILE_ROWS
    n_tiles = rows // tm
    base = (layer * N_MOD + 3) * MOD_ROWS
    x_spec, a_spec, s_spec, g_spec, out_spec = _skewed_specs(tm, d, n_tiles, seq // tm, base)
    return pl.pallas_call(
        _ffn_kernel,
        out_shape=jax.ShapeDtypeStruct((rows, d), _F32),
        grid=(n_tiles + 1,),
        in_specs=[x_spec, out_spec, a_spec, s_spec, g_spec,
                  _resident_layer(w_in.shape, layer), _resident_layer(w_out.shape, layer)],
        out_specs=out_spec,
        scratch_shapes=[pltpu.VMEM((tm, FFN_WIDTH), _BF16)],
        compiler_params=_compiler_params(),
        name=f"ffn_sublayer_{layer}",
    )(x, x, mod, mod, mod, w_in, w_out)


def _gelu_tanh(z):
    c = math.sqrt(2.0 / math.pi)
    return 0.5 * z * (1.0 + jnp.tanh(c * (z + 0.044715 * (z * z * z))))


def _sg_kernel(x_ref, a_ref, s_ref, g_ref, w_in_ref, ln_g_ref, ln_b_ref, w_s_ref,
               b_s_ref, w_out_ref, o_ref, z_ref, u_ref, vn_ref, y_ref):
    tm = x_ref.shape[0]
    x = x_ref[...]
    z_ref[...] = _dot(_prenorm(x, a_ref, s_ref, chained=False), w_in_ref[...])

    for p in range(tm // SG_PIECE_ROWS):
        rows = slice(p * SG_PIECE_ROWS, (p + 1) * SG_PIECE_ROWS)
        u_ref[rows, :] = _gelu_tanh(z_ref[rows, :SG_WIDTH])
        v = _gelu_tanh(z_ref[rows, SG_WIDTH:])
        mu = jnp.mean(v, axis=-1, keepdims=True)
        vc = v - mu
        var = jnp.mean(vc * vc, axis=-1, keepdims=True)
        vn = vc * lax.rsqrt(var + LN_EPS) * ln_g_ref[...] + ln_b_ref[...]
        vn_ref[rows, :] = vn.astype(_BF16)

    row = lax.broadcasted_iota(jnp.int32, (CHUNK, CHUNK), 0)
    col = lax.broadcasted_iota(jnp.int32, (CHUNK, CHUNK), 1)
    causal = row >= col
    for g in range(SG_GROUPS):
        w_g = jnp.where(causal, w_s_ref[g], 0.0).astype(_BF16)
        b_g = jnp.broadcast_to(b_s_ref[:, g:g + 1], (CHUNK, SG_GROUP_DIM))
        cols = slice(g * SG_GROUP_DIM, (g + 1) * SG_GROUP_DIM)
        for n in range(tm // CHUNK):
            rows = slice(n * CHUNK, (n + 1) * CHUNK)
            mixed = _dot(w_g, vn_ref[rows, cols]) + b_g
            y_ref[rows, cols] = (u_ref[rows, cols] * mixed).astype(_BF16)

    y = _dot(y_ref[...], w_out_ref[...])
    o_ref[...] = _residual(x, y, g_ref)


def _sg_sublayer(x, mod, layer, index, w_in, ln_g, ln_b, w_s, b_s_t, w_out, seq):
    rows, d = x.shape
    tm = SG_TILE_ROWS
    n_tiles = rows // tm
    base = layer * N_MOD * MOD_ROWS
    x_spec, a_spec, s_spec, g_spec, out_spec = _skewed_specs(tm, d, n_tiles, seq // tm, base,
                                                             lag=0)
    return pl.pallas_call(
        _sg_kernel,
        out_shape=jax.ShapeDtypeStruct((rows, d), _F32),
        grid=(n_tiles,),
        in_specs=[
            x_spec, a_spec, s_spec, g_spec,
            _resident_layer(w_in.shape, index),
            _resident_layer(ln_g.shape, index),
            _resident_layer(ln_b.shape, index),
            _resident_layer(w_s.shape, index),
            _resident_layer(b_s_t.shape, index),
            _resident_layer(w_out.shape, index),
        ],
        out_specs=out_spec,
        scratch_shapes=[
            pltpu.VMEM((tm, 2 * SG_WIDTH), _F32),
            pltpu.VMEM((tm, SG_WIDTH), _F32),
            pltpu.VMEM((tm, SG_WIDTH), _BF16),
            pltpu.VMEM((tm, SG_WIDTH), _BF16),
        ],
        compiler_params=_compiler_params(),
        name=f"sg_sublayer_{layer}",
    )(x, mod, mod, mod, w_in, ln_g, ln_b, w_s, b_s_t, w_out)


def _rotate(t, cos, sin):
    half = RET_DK // 2
    t1 = t[:, :half]
    t2 = t[:, half:]
    return jnp.concatenate([t1 * cos - t2 * sin, t1 * sin + t2 * cos], axis=-1)


def _retention_head(q, k, v, hd, state_ref, d_intra_ref, d_q_ref, d_k_ref, d_c_ref, ret_ref):
    state = state_ref[hd]
    d_intra = d_intra_ref[hd]
    for first in range(0, q.shape[0] // CHUNK, RET_CHUNK_WINDOW):
        window = range(first, first + RET_CHUNK_WINDOW)
        scores, updates = {}, {}
        for n in window:
            rows = slice(n * CHUNK, (n + 1) * CHUNK)
            scores[n] = lax.dot_general(q[rows].astype(_BF16), k[rows].astype(_BF16),
                                        (((1,), (1,)), ((), ())), preferred_element_type=_F32)
            kd = (k[rows] * d_k_ref[hd]).astype(_BF16)
            updates[n] = lax.dot_general(kd, v[rows], (((0,), (0,)), ((), ())),
                                         preferred_element_type=_F32)
        for n in window:
            rows = slice(n * CHUNK, (n + 1) * CHUNK)
            masked = (scores[n] * d_intra).astype(_BF16)
            qd = (q[rows] * d_q_ref[hd]).astype(_BF16)
            out = _dot(masked, v[rows]) + _dot(qd, state.astype(_BF16))
            ret_ref[rows, hd * RET_DV:(hd + 1) * RET_DV] = out
            state = state * d_c_ref[hd] + updates[n]
    state_ref[hd] = state


def _ret_kernel(x_ref, a_ref, s_ref, g_ref, w_in_ref, cos_ref, sin_ref, d_intra_ref, d_q_ref,
                d_k_ref, d_c_ref, gn_g_ref, gn_b_ref, w_out_ref, o_ref, state_ref, ret_ref,
                *, tiles_per_batch):
    @pl.when(pl.program_id(0) % tiles_per_batch == 0)
    def _():
        state_ref[...] = jnp.zeros_like(state_ref)

    x = x_ref[...]
    h = _prenorm(x, a_ref, s_ref, chained=False)
    cos = cos_ref[...]
    sin = sin_ref[...]

    def project(lo, width):
        return _dot(h, w_in_ref[:, lo:lo + width])

    def project_head(hd):
        q = _rotate(project(hd * RET_DK, RET_DK), cos, sin)
        k = _rotate(project(RET_QK_WIDTH + hd * RET_DK, RET_DK), cos, sin)
        v = project(2 * RET_QK_WIDTH + hd * RET_DV, RET_DV).astype(_BF16)
        return q, k, v

    def retain(hd, qkv):
        _retention_head(*qkv, hd, state_ref, d_intra_ref, d_q_ref, d_k_ref, d_c_ref, ret_ref)

    gate_lo = 2 * RET_QK_WIDTH + RET_V_WIDTH
    half = RET_V_WIDTH // 2

    qkv0 = project_head(0)
    qkv1 = project_head(1)
    retain(0, qkv0)
    qkv2 = project_head(2)
    retain(1, qkv1)
    qkv3 = project_head(3)
    retain(2, qkv2)
    gates = [project(gate_lo, half)]
    retain(3, qkv3)
    gates.append(project(gate_lo + half, half))

    for hd in range(RET_HEADS):
        cols = slice(hd * RET_DV, (hd + 1) * RET_DV)
        o = ret_ref[:, cols]
        mu = jnp.mean(o, axis=-1, keepdims=True)
        oc = o - mu
        var = jnp.mean(oc * oc, axis=-1, keepdims=True)
        on = oc * lax.rsqrt(var + LN_EPS) * gn_g_ref[:, cols] + gn_b_ref[:, cols]
        gate = gates[hd // 2][:, (hd % 2) * RET_DV:(hd % 2 + 1) * RET_DV]
        ret_ref[:, cols] = on * (gate * jax.nn.sigmoid(gate))

    y = _dot(ret_ref[...].astype(_BF16), w_out_ref[...])
    o_ref[...] = _residual(x, y, g_ref)


def _retention_tables(seq):
    half = RET_DK // 2
    inv_freq = ROPE_BASE ** (-jnp.arange(half, dtype=_F32) / half)
    pos = jnp.arange(seq, dtype=_F32)
    ang = pos[:, None] * inv_freq[None, :]
    cos = jnp.cos(ang)
    sin = jnp.sin(ang)
    log_gamma = jnp.log(1.0 - jnp.exp2(-5.0 - jnp.arange(RET_HEADS, dtype=_F32)))
    idx = jnp.arange(CHUNK, dtype=_F32)
    diff = idx[:, None] - idx[None, :]
    d_intra = jnp.where(diff >= 0,
                        jnp.exp(log_gamma[:, None, None] * jnp.maximum(diff, 0.0)), 0.0)
    d_q = jnp.exp(log_gamma[:, None] * (idx + 1.0))
    d_k = jnp.exp(log_gamma[:, None] * (CHUNK - 1.0 - idx))
    d_c = jnp.exp(log_gamma * CHUNK)
    k_scale = RET_DK ** -0.5
    assert math.log2(k_scale).is_integer()
    d_intra = d_intra * k_scale
    d_k = d_k * k_scale
    d_q = jnp.broadcast_to(d_q[:, :, None], (RET_HEADS, CHUNK, RET_DK))
    d_k = jnp.broadcast_to(d_k[:, :, None], (RET_HEADS, CHUNK, RET_DK))
    d_c = jnp.broadcast_to(d_c[:, None, None], (RET_HEADS, 1, RET_DV))
    return cos, sin, d_intra, d_q, d_k, d_c


def _ret_sublayer(x, mod, layer, index, w_in, gn_g, gn_b, w_out, tables, seq):
    rows, d = x.shape
    tm = RET_TILE_ROWS
    n_tiles = rows // tm
    tpb = seq // tm
    base = layer * N_MOD * MOD_ROWS
    cos, sin, d_intra, d_q, d_k, d_c = tables
    x_spec, a_spec, s_spec, g_spec, out_spec = _skewed_specs(tm, d, n_tiles, tpb, base, lag=0)
    pos_spec = pl.BlockSpec((tm, RET_DK // 2), lambda i: (i % tpb, 0))
    return pl.pallas_call(
        functools.partial(_ret_kernel, tiles_per_batch=tpb),
        out_shape=jax.ShapeDtypeStruct((rows, d), _F32),
        grid=(n_tiles,),
        in_specs=[
            x_spec, a_spec, s_spec, g_spec,
            _resident_layer(w_in.shape, index),
            pos_spec,
            pos_spec,
            _resident(d_intra.shape),
            _resident(d_q.shape),
            _resident(d_k.shape),
            _resident(d_c.shape),
            _resident_layer(gn_g.shape, index),
            _resident_layer(gn_b.shape, index),
            _resident_layer(w_out.shape, index),
        ],
        out_specs=out_spec,
        scratch_shapes=[
            pltpu.VMEM((RET_HEADS, RET_DK, RET_DV), _F32),
            pltpu.VMEM((tm, RET_V_WIDTH), _F32),
        ],
        compiler_params=_compiler_params(),
        name=f"ret_sublayer_{layer}",
    )(x, mod, mod, mod, w_in, cos, sin, d_intra, d_q, d_k, d_c, gn_g, gn_b, w_out)


def kernel(x, c, ada_w, ada_b, pre_mix_g, post_mix_g, pre_ffn_g, post_ffn_g, ffn_w_in, ffn_w_out,
           sg_w_in, sg_ln_g, sg_ln_b, sg_w_s, sg_b_s, sg_w_out,
           ret_w_in, ret_gn_g, ret_gn_b, ret_w_out):
    batch, seq, d = x.shape
    depth = ada_w.shape[0]
    assert d == D_MODEL and batch <= MOD_ROWS
    assert seq % max(FFN_TILE_ROWS, SG_TILE_ROWS, RET_TILE_ROWS) == 0

    ones = jnp.ones_like(pre_mix_g)
    gains = jnp.stack([ones, pre_mix_g, post_mix_g, ones, pre_ffn_g, post_ffn_g], axis=1)
    mod = _ada_modulation(c, ada_w, ada_b, gains.reshape(depth, N_MOD, 1, d))
    mod = mod.reshape(depth * N_MOD * MOD_ROWS, 1, d)

    ffn_w = (ffn_w_in.astype(_BF16), ffn_w_out.astype(_BF16))
    sg_params = (sg_w_in.astype(_BF16), sg_ln_g[:, None, :], sg_ln_b[:, None, :], sg_w_s,
                 jnp.swapaxes(sg_b_s, 1, 2), sg_w_out.astype(_BF16))
    ret_params = (ret_w_in.astype(_BF16), ret_gn_g[:, None, :], ret_gn_b[:, None, :],
                  ret_w_out.astype(_BF16))

    tables = _retention_tables(seq)
    xf = x.reshape(batch * seq, d)
    for i in range(depth):
        j = i // 2
        if i % 2 == 0:
            xf = _sg_sublayer(xf, mod, i, j, *sg_params, seq)
        else:
            xf = _ret_sublayer(xf, mod, i, j, *ret_params, tables, seq)
        xf = _ffn_sublayer(xf, mod, i, *ffn_w, seq)
    return xf.reshape(batch, seq, d)
```

```python
import functools
import math

import jax
import jax.numpy as jnp
from jax import lax
from jax.experimental import pallas as pl
from jax.experimental.pallas import tpu as pltpu

D_MODEL = 1024
CHUNK = 128
SG_WIDTH = 2 * D_MODEL
SG_GROUPS = 8
SG_GROUP_DIM = SG_WIDTH // SG_GROUPS
RET_HEADS = 4
RET_DK = 256
RET_DV = 512
RET_QK_WIDTH = RET_HEADS * RET_DK
RET_V_WIDTH = RET_HEADS * RET_DV
RET_IN_WIDTH = 2 * RET_QK_WIDTH + 2 * RET_V_WIDTH
ROPE_BASE = 10000.0
FFN_WIDTH = 4 * D_MODEL
N_MOD = 6
RMS_EPS = 1e-6
LN_EPS = 1e-5

VMEM_LIMIT_BYTES_V7X = 56 * 1024 * 1024
MOD_ROWS = 8

FFN_TILE_ROWS = 1024
SG_TILE_ROWS = 512
RET_TILE_ROWS = 512
RET_BLOCK = 256
RET_BLOCK_WINDOW = 2
SG_PIECE_ROWS = 16

_BF16 = jnp.bfloat16
_F32 = jnp.float32


def _compiler_params():
    return pltpu.CompilerParams(
        dimension_semantics=("arbitrary",),
        vmem_limit_bytes=VMEM_LIMIT_BYTES_V7X,
    )


def _resident(shape):
    zeros = (0,) * len(shape)
    return pl.BlockSpec(shape, lambda i: zeros, pipeline_mode=pl.Buffered(1))


def _resident_layer(stacked_shape, index):
    block_index = (index,) + (0,) * (len(stacked_shape) - 1)
    return pl.BlockSpec((None,) + tuple(stacked_shape[1:]), lambda i: block_index,
                        pipeline_mode=pl.Buffered(1))


def _rms_scale(v):
    return lax.rsqrt(jnp.mean(v * v, axis=-1, keepdims=True) + RMS_EPS)


def _dot(a, b):
    return jnp.dot(a, b, preferred_element_type=_F32)


def _ada_kernel(c_ref, w_ref, b_ref, g_ref, o_ref):
    d = c_ref.shape[1]
    c = c_ref[...]
    c_act = c * jax.nn.sigmoid(c)
    raw = _dot(c_act.astype(_BF16), w_ref[...].astype(_BF16))
    for k, one in enumerate((0.0, 1.0, 0.0)):
        o_ref[k] = (raw[:, k * d:(k + 1) * d] + b_ref[k] + one) * g_ref[k]


def _ada_modulation(c, ada_w, ada_b, gains):
    depth = ada_w.shape[0]
    batch = c.shape[0]
    d = D_MODEL
    per_step = N_MOD // 2
    c_pad = jnp.zeros((MOD_ROWS, d), _F32).at[:batch].set(c)
    b4 = ada_b.reshape(depth, N_MOD, 1, d)
    return pl.pallas_call(
        _ada_kernel,
        out_shape=jax.ShapeDtypeStruct((depth, N_MOD, MOD_ROWS, d), _F32),
        grid=(depth, 2),
        in_specs=[
            pl.BlockSpec((MOD_ROWS, d), lambda i, j: (0, 0)),
            pl.BlockSpec((None, d, per_step * d), lambda i, j: (i, 0, j)),
            pl.BlockSpec((None, per_step, 1, d), lambda i, j: (i, j, 0, 0)),
            pl.BlockSpec((None, per_step, 1, d), lambda i, j: (i, j, 0, 0)),
        ],
        out_specs=pl.BlockSpec((None, per_step, MOD_ROWS, d), lambda i, j: (i, j, 0, 0)),
        compiler_params=pltpu.CompilerParams(
            dimension_semantics=("arbitrary", "arbitrary"),
            vmem_limit_bytes=VMEM_LIMIT_BYTES_V7X,
        ),
        name="ada_modulation",
    )(c_pad, ada_w, b4, gains)


def _prenorm(x, a_ref, s_ref):
    return (x * _rms_scale(x) * a_ref[...] + s_ref[...]).astype(_BF16)


def _residual(x, y, g_ref):
    return x + y * _rms_scale(y) * g_ref[...]


def _ffn_kernel(x_ref, a_ref, s_ref, g_ref, w_in_ref, w_out_ref, o_ref, r_ref):
    x = x_ref[...]
    a = _dot(_prenorm(x, a_ref, s_ref), w_in_ref[...])
    r = jnp.maximum(a, 0.0)
    r_ref[...] = (r * r).astype(_BF16)
    y = _dot(r_ref[...], w_out_ref[...])
    o_ref[...] = _residual(x, y, g_ref)


def _tile_specs(tm, d, tpb, base):
    def mod_spec(row_base):
        return pl.BlockSpec((None, 1, d), lambda i: (row_base + i // tpb, 0, 0))

    row_spec = pl.BlockSpec((tm, d), lambda i: (i, 0))
    return (row_spec,
            mod_spec(base + MOD_ROWS),
            mod_spec(base),
            mod_spec(base + 2 * MOD_ROWS),
            row_spec)


def _ffn_sublayer(x, mod, layer, w_in, w_out, seq):
    rows, d = x.shape
    tm = FFN_TILE_ROWS
    n_tiles = rows // tm
    base = (layer * N_MOD + 3) * MOD_ROWS
    x_spec, a_spec, s_spec, g_spec, out_spec = _tile_specs(tm, d, seq // tm, base)
    return pl.pallas_call(
        _ffn_kernel,
        out_shape=jax.ShapeDtypeStruct((rows, d), _F32),
        grid=(n_tiles,),
        in_specs=[x_spec, a_spec, s_spec, g_spec,
                  _resident_layer(w_in.shape, layer), _resident_layer(w_out.shape, layer)],
        out_specs=out_spec,
        scratch_shapes=[pltpu.VMEM((tm, FFN_WIDTH), _BF16)],
        compiler_params=_compiler_params(),
        name=f"ffn_sublayer_{layer}",
    )(x, mod, mod, mod, w_in, w_out)


def _gelu_tanh(z):
    c = math.sqrt(2.0 / math.pi)
    return 0.5 * z * (1.0 + jnp.tanh(c * (z + 0.044715 * (z * z * z))))


def _sg_kernel(x_ref, a_ref, s_ref, g_ref, w_in_ref, ln_g_ref, ln_b_ref, w_s_ref,
               b_s_ref, w_out_ref, o_ref, z_ref, u_ref, vn_ref, y_ref):
    tm = x_ref.shape[0]
    x = x_ref[...]
    z_ref[...] = _dot(_prenorm(x, a_ref, s_ref), w_in_ref[...])

    for p in range(tm // SG_PIECE_ROWS):
        rows = slice(p * SG_PIECE_ROWS, (p + 1) * SG_PIECE_ROWS)
        u_ref[rows, :] = _gelu_tanh(z_ref[rows, :SG_WIDTH])
        v = _gelu_tanh(z_ref[rows, SG_WIDTH:])
        mu = jnp.mean(v, axis=-1, keepdims=True)
        vc = v - mu
        var = jnp.mean(vc * vc, axis=-1, keepdims=True)
        vn = vc * lax.rsqrt(var + LN_EPS) * ln_g_ref[...] + ln_b_ref[...]
        vn_ref[rows, :] = vn.astype(_BF16)

    row = lax.broadcasted_iota(jnp.int32, (CHUNK, CHUNK), 0)
    col = lax.broadcasted_iota(jnp.int32, (CHUNK, CHUNK), 1)
    causal = row >= col
    for g in range(SG_GROUPS):
        w_g = jnp.where(causal, w_s_ref[g], 0.0).astype(_BF16)
        b_g = jnp.broadcast_to(b_s_ref[:, g:g + 1], (CHUNK, SG_GROUP_DIM))
        cols = slice(g * SG_GROUP_DIM, (g + 1) * SG_GROUP_DIM)
        for n in range(tm // CHUNK):
            rows = slice(n * CHUNK, (n + 1) * CHUNK)
            mixed = _dot(w_g, vn_ref[rows, cols]) + b_g
            y_ref[rows, cols] = (u_ref[rows, cols] * mixed).astype(_BF16)

    y = _dot(y_ref[...], w_out_ref[...])
    o_ref[...] = _residual(x, y, g_ref)


def _sg_sublayer(x, mod, layer, index, w_in, ln_g, ln_b, w_s, b_s_t, w_out, seq):
    rows, d = x.shape
    tm = SG_TILE_ROWS
    n_tiles = rows // tm
    base = layer * N_MOD * MOD_ROWS
    x_spec, a_spec, s_spec, g_spec, out_spec = _tile_specs(tm, d, seq // tm, base)
    return pl.pallas_call(
        _sg_kernel,
        out_shape=jax.ShapeDtypeStruct((rows, d), _F32),
        grid=(n_tiles,),
        in_specs=[
            x_spec, a_spec, s_spec, g_spec,
            _resident_layer(w_in.shape, index),
            _resident_layer(ln_g.shape, index),
            _resident_layer(ln_b.shape, index),
            _resident_layer(w_s.shape, index),
            _resident_layer(b_s_t.shape, index),
            _resident_layer(w_out.shape, index),
        ],
        out_specs=out_spec,
        scratch_shapes=[
            pltpu.VMEM((tm, 2 * SG_WIDTH), _F32),
            pltpu.VMEM((tm, SG_WIDTH), _F32),
            pltpu.VMEM((tm, SG_WIDTH), _BF16),
            pltpu.VMEM((tm, SG_WIDTH), _BF16),
        ],
        compiler_params=_compiler_params(),
        name=f"sg_sublayer_{layer}",
    )(x, mod, mod, mod, w_in, ln_g, ln_b, w_s, b_s_t, w_out)


def _rotate(t, cos, sin):
    half = RET_DK // 2
    t1 = t[:, :half]
    t2 = t[:, half:]
    return jnp.concatenate([t1 * cos - t2 * sin, t1 * sin + t2 * cos], axis=-1)


def _retention_head(q, k, v, hd, state_ref, d_intra_ref, d_q_ref, d_k_ref, d_c_ref, ret_ref):
    state = state_ref[hd]
    d_intra = d_intra_ref[hd]
    for first in range(0, q.shape[0] // RET_BLOCK, RET_BLOCK_WINDOW):
        window = range(first, first + RET_BLOCK_WINDOW)
        scores, updates = {}, {}
        for n in window:
            rows = slice(n * RET_BLOCK, (n + 1) * RET_BLOCK)
            scores[n] = lax.dot_general(q[rows].astype(_BF16), k[rows].astype(_BF16),
                                        (((1,), (1,)), ((), ())), preferred_element_type=_F32)
            kd = (k[rows] * d_k_ref[hd]).astype(_BF16)
            updates[n] = lax.dot_general(kd, v[rows], (((0,), (0,)), ((), ())),
                                         preferred_element_type=_F32)
        for n in window:
            rows = slice(n * RET_BLOCK, (n + 1) * RET_BLOCK)
            masked = (scores[n] * d_intra).astype(_BF16)
            qd = (q[rows] * d_q_ref[hd]).astype(_BF16)
            out = _dot(masked, v[rows]) + _dot(qd, state.astype(_BF16))
            ret_ref[rows, hd * RET_DV:(hd + 1) * RET_DV] = out
            state = state * d_c_ref[hd] + updates[n]
    state_ref[hd] = state


def _ret_kernel(x_ref, a_ref, s_ref, g_ref, w_in_ref, cos_ref, sin_ref, d_intra_ref, d_q_ref,
                d_k_ref, d_c_ref, gn_g_ref, gn_b_ref, w_out_ref, o_ref, state_ref, ret_ref,
                *, tiles_per_batch):
    @pl.when(pl.program_id(0) % tiles_per_batch == 0)
    def _():
        state_ref[...] = jnp.zeros_like(state_ref)

    x = x_ref[...]
    h = _prenorm(x, a_ref, s_ref)
    cos = cos_ref[...]
    sin = sin_ref[...]

    def project(lo, width):
        return _dot(h, w_in_ref[:, lo:lo + width])

    def project_head(hd):
        q = _rotate(project(hd * RET_DK, RET_DK), cos, sin)
        k = _rotate(project(RET_QK_WIDTH + hd * RET_DK, RET_DK), cos, sin)
        v = project(2 * RET_QK_WIDTH + hd * RET_DV, RET_DV).astype(_BF16)
        return q, k, v

    def retain(hd, qkv):
        _retention_head(*qkv, hd, state_ref, d_intra_ref, d_q_ref, d_k_ref, d_c_ref, ret_ref)

    gate_lo = 2 * RET_QK_WIDTH + RET_V_WIDTH
    half = RET_V_WIDTH // 2

    qkv0 = project_head(0)
    qkv1 = project_head(1)
    retain(0, qkv0)
    qkv2 = project_head(2)
    retain(1, qkv1)
    qkv3 = project_head(3)
    retain(2, qkv2)
    gates = [project(gate_lo, half)]
    retain(3, qkv3)
    gates.append(project(gate_lo + half, half))

    for hd in range(RET_HEADS):
        cols = slice(hd * RET_DV, (hd + 1) * RET_DV)
        o = ret_ref[:, cols]
        mu = jnp.mean(o, axis=-1, keepdims=True)
        oc = o - mu
        var = jnp.mean(oc * oc, axis=-1, keepdims=True)
        on = oc * lax.rsqrt(var + LN_EPS) * gn_g_ref[:, cols] + gn_b_ref[:, cols]
        gate = gates[hd // 2][:, (hd % 2) * RET_DV:(hd % 2 + 1) * RET_DV]
        ret_ref[:, cols] = on * (gate * jax.nn.sigmoid(gate))

    y = _dot(ret_ref[...].astype(_BF16), w_out_ref[...])
    o_ref[...] = _residual(x, y, g_ref)


def _retention_tables(seq):
    half = RET_DK // 2
    inv_freq = ROPE_BASE ** (-jnp.arange(half, dtype=_F32) / half)
    pos = jnp.arange(seq, dtype=_F32)
    ang = pos[:, None] * inv_freq[None, :]
    cos = jnp.cos(ang)
    sin = jnp.sin(ang)
    log_gamma = jnp.log(1.0 - jnp.exp2(-5.0 - jnp.arange(RET_HEADS, dtype=_F32)))
    idx = jnp.arange(RET_BLOCK, dtype=_F32)
    diff = idx[:, None] - idx[None, :]
    d_intra = jnp.where(diff >= 0,
                        jnp.exp(log_gamma[:, None, None] * jnp.maximum(diff, 0.0)), 0.0)
    d_q = jnp.exp(log_gamma[:, None] * (idx + 1.0))
    d_k = jnp.exp(log_gamma[:, None] * (RET_BLOCK - 1.0 - idx))
    d_c = jnp.exp(log_gamma * RET_BLOCK)
    k_scale = RET_DK ** -0.5
    assert math.log2(k_scale).is_integer()
    d_intra = d_intra * k_scale
    d_k = d_k * k_scale
    d_q = jnp.broadcast_to(d_q[:, :, None], (RET_HEADS, RET_BLOCK, RET_DK))
    d_k = jnp.broadcast_to(d_k[:, :, None], (RET_HEADS, RET_BLOCK, RET_DK))
    d_c = jnp.broadcast_to(d_c[:, None, None], (RET_HEADS, 1, RET_DV))
    return cos, sin, d_intra, d_q, d_k, d_c


def _ret_sublayer(x, mod, layer, index, w_in, gn_g, gn_b, w_out, tables, seq):
    rows, d = x.shape
    tm = RET_TILE_ROWS
    n_tiles = rows // tm
    tpb = seq // tm
    base = layer * N_MOD * MOD_ROWS
    cos, sin, d_intra, d_q, d_k, d_c = tables
    x_spec, a_spec, s_spec, g_spec, out_spec = _tile_specs(tm, d, tpb, base)
    pos_spec = pl.BlockSpec((tm, RET_DK // 2), lambda i: (i % tpb, 0))
    return pl.pallas_call(
        functools.partial(_ret_kernel, tiles_per_batch=tpb),
        out_shape=jax.ShapeDtypeStruct((rows, d), _F32),
        grid=(n_tiles,),
        in_specs=[
            x_spec, a_spec, s_spec, g_spec,
            _resident_layer(w_in.shape, index),
            pos_spec,
            pos_spec,
            _resident(d_intra.shape),
            _resident(d_q.shape),
            _resident(d_k.shape),
            _resident(d_c.shape),
            _resident_layer(gn_g.shape, index),
            _resident_layer(gn_b.shape, index),
            _resident_layer(w_out.shape, index),
        ],
        out_specs=out_spec,
        scratch_shapes=[
            pltpu.VMEM((RET_HEADS, RET_DK, RET_DV), _F32),
            pltpu.VMEM((tm, RET_V_WIDTH), _F32),
        ],
        compiler_params=_compiler_params(),
        name=f"ret_sublayer_{layer}",
    )(x, mod, mod, mod, w_in, cos, sin, d_intra, d_q, d_k, d_c, gn_g, gn_b, w_out)


def kernel(x, c, ada_w, ada_b, pre_mix_g, post_mix_g, pre_ffn_g, post_ffn_g, ffn_w_in, ffn_w_out,
           sg_w_in, sg_ln_g, sg_ln_b, sg_w_s, sg_b_s, sg_w_out,
           ret_w_in, ret_gn_g, ret_gn_b, ret_w_out):
    batch, seq, d = x.shape
    depth = ada_w.shape[0]
    assert d == D_MODEL and batch <= MOD_ROWS
    assert seq % max(FFN_TILE_ROWS, SG_TILE_ROWS, RET_TILE_ROWS) == 0

    ones = jnp.ones_like(pre_mix_g)
    gains = jnp.stack([ones, pre_mix_g, post_mix_g, ones, pre_ffn_g, post_ffn_g], axis=1)
    mod = _ada_modulation(c, ada_w, ada_b, gains.reshape(depth, N_MOD, 1, d))
    mod = mod.reshape(depth * N_MOD * MOD_ROWS, 1, d)

    ffn_w = (ffn_w_in.astype(_BF16), ffn_w_out.astype(_BF16))
    sg_params = (sg_w_in.astype(_BF16), sg_ln_g[:, None, :], sg_ln_b[:, None, :], sg_w_s,
                 jnp.swapaxes(sg_b_s, 1, 2), sg_w_out.astype(_BF16))
    ret_params = (ret_w_in.astype(_BF16), ret_gn_g[:, None, :], ret_gn_b[:, None, :],
                  ret_w_out.astype(_BF16))

    tables = _retention_tables(seq)
    xf = x.reshape(batch * seq, d)
    for i in range(depth):
        j = i // 2
        if i % 2 == 0:
            xf = _sg_sublayer(xf, mod, i, j, *sg_params, seq)
        else:
            xf = _ret_sublayer(xf, mod, i, j, *ret_params, tables, seq)
        xf = _ffn_sublayer(xf, mod, i, *ffn_w, seq)
    return xf.reshape(batch, seq, d)
```

```python
import functools
import math

import jax
import jax.numpy as jnp
from jax import lax
from jax.experimental import pallas as pl
from jax.experimental.pallas import tpu as pltpu

D_MODEL = 1024
CHUNK = 128
SG_WIDTH = 2 * D_MODEL
SG_GROUPS = 8
SG_GROUP_DIM = SG_WIDTH // SG_GROUPS
RET_HEADS = 4
RET_DK = 256
RET_DV = 512
RET_QK_WIDTH = RET_HEADS * RET_DK
RET_V_WIDTH = RET_HEADS * RET_DV
RET_IN_WIDTH = 2 * RET_QK_WIDTH + 2 * RET_V_WIDTH
ROPE_BASE = 10000.0
FFN_WIDTH = 4 * D_MODEL
N_MOD = 6
RMS_EPS = 1e-6
LN_EPS = 1e-5

VMEM_LIMIT_BYTES_V7X = 56 * 1024 * 1024
MOD_ROWS = 8

FFN_TILE_ROWS = 1024
SG_TILE_ROWS = 512
RET_TILE_ROWS = 512
RET_BLOCK = 256
RET_BLOCK_WINDOW = 2
SG_PIECE_ROWS = 16
BF16_SUBLANE_TILE = 16

_BF16 = jnp.bfloat16
_F32 = jnp.float32


def _compiler_params():
    return pltpu.CompilerParams(
        dimension_semantics=("arbitrary",),
        vmem_limit_bytes=VMEM_LIMIT_BYTES_V7X,
    )


def _resident(shape):
    zeros = (0,) * len(shape)
    return pl.BlockSpec(shape, lambda i: zeros, pipeline_mode=pl.Buffered(1))


def _resident_layer(stacked_shape, index):
    block_index = (index,) + (0,) * (len(stacked_shape) - 1)
    return pl.BlockSpec((None,) + tuple(stacked_shape[1:]), lambda i: block_index,
                        pipeline_mode=pl.Buffered(1))


def _rms_scale(v):
    return lax.rsqrt(jnp.mean(v * v, axis=-1, keepdims=True) + RMS_EPS)


def _dot(a, b):
    return jnp.dot(a, b, preferred_element_type=_F32)


def _next_weight_plumbing(next_weights, n_steps):
    in_specs, args, out_specs, out_shapes = [], [], [], []
    for stacked, index in next_weights:
        _, rows, width = stacked.shape
        assert rows % n_steps == 0 and (rows // n_steps) % BF16_SUBLANE_TILE == 0
        slice_rows = rows // n_steps
        in_specs.append(pl.BlockSpec((None, slice_rows, width),
                                     lambda i, index=index: (index, i, 0)))
        args.append(stacked)
        out_specs.append(pl.BlockSpec((slice_rows, width), lambda i: (i, 0)))
        out_shapes.append(jax.ShapeDtypeStruct((rows, width), _BF16))
    return in_specs, args, out_specs, out_shapes


def _cast_next_weights(refs, n_inputs, n_next):
    inputs = refs[:n_inputs]
    f32_slices = refs[n_inputs:n_inputs + n_next]
    output = refs[n_inputs + n_next]
    bf16_slices = refs[n_inputs + n_next + 1:n_inputs + 2 * n_next + 1]
    scratch = refs[n_inputs + 2 * n_next + 1:]
    for src, dst in zip(f32_slices, bf16_slices):
        dst[...] = src[...].astype(_BF16)
    return inputs + (output,) + scratch


def _ada_kernel(c_ref, w_ref, b_ref, g_ref, o_ref):
    d = c_ref.shape[1]
    c = c_ref[...]
    c_act = c * jax.nn.sigmoid(c)
    raw = _dot(c_act.astype(_BF16), w_ref[...].astype(_BF16))
    for k, one in enumerate((0.0, 1.0, 0.0)):
        o_ref[k] = (raw[:, k * d:(k + 1) * d] + b_ref[k] + one) * g_ref[k]


def _ada_modulation(c, ada_w, ada_b, gains):
    depth = ada_w.shape[0]
    batch = c.shape[0]
    d = D_MODEL
    per_step = N_MOD // 2
    c_pad = jnp.zeros((MOD_ROWS, d), _F32).at[:batch].set(c)
    b4 = ada_b.reshape(depth, N_MOD, 1, d)
    return pl.pallas_call(
        _ada_kernel,
        out_shape=jax.ShapeDtypeStruct((depth, N_MOD, MOD_ROWS, d), _F32),
        grid=(depth, 2),
        in_specs=[
            pl.BlockSpec((MOD_ROWS, d), lambda i, j: (0, 0)),
            pl.BlockSpec((None, d, per_step * d), lambda i, j: (i, 0, j)),
            pl.BlockSpec((None, per_step, 1, d), lambda i, j: (i, j, 0, 0)),
            pl.BlockSpec((None, per_step, 1, d), lambda i, j: (i, j, 0, 0)),
        ],
        out_specs=pl.BlockSpec((None, per_step, MOD_ROWS, d), lambda i, j: (i, j, 0, 0)),
        compiler_params=pltpu.CompilerParams(
            dimension_semantics=("arbitrary", "arbitrary"),
            vmem_limit_bytes=VMEM_LIMIT_BYTES_V7X,
        ),
        name="ada_modulation",
    )(c_pad, ada_w, b4, gains)


def _prenorm(x, a_ref, s_ref):
    return (x * _rms_scale(x) * a_ref[...] + s_ref[...]).astype(_BF16)


def _residual(x, y, g_ref):
    return x + y * _rms_scale(y) * g_ref[...]


def _tile_specs(tm, d, tpb, base):
    def mod_spec(row_base):
        return pl.BlockSpec((None, 1, d), lambda i: (row_base + i // tpb, 0, 0))

    row_spec = pl.BlockSpec((tm, d), lambda i: (i, 0))
    return (row_spec,
            mod_spec(base + MOD_ROWS),
            mod_spec(base),
            mod_spec(base + 2 * MOD_ROWS),
            row_spec)


def _sublayer_call(kernel, n_inputs, name, x, tm, in_specs, args, out_spec, scratch_shapes,
                   next_weights):
    rows, d = x.shape
    n_tiles = rows // tm
    nw_in_specs, nw_args, nw_out_specs, nw_out_shapes = _next_weight_plumbing(next_weights,
                                                                             n_tiles)
    assert len(in_specs) == len(args) == n_inputs
    return pl.pallas_call(
        functools.partial(kernel, n_inputs=n_inputs, n_next=len(next_weights)),
        out_shape=[jax.ShapeDtypeStruct((rows, d), _F32)] + nw_out_shapes,
        grid=(n_tiles,),
        in_specs=list(in_specs) + nw_in_specs,
        out_specs=[out_spec] + nw_out_specs,
        scratch_shapes=scratch_shapes,
        compiler_params=_compiler_params(),
        name=name,
    )(*args, *nw_args)


def _ffn_kernel(*refs, n_inputs, n_next):
    x_ref, a_ref, s_ref, g_ref, w_in_ref, w_out_ref, o_ref, r_ref = _cast_next_weights(
        refs, n_inputs, n_next)
    x = x_ref[...]
    a = _dot(_prenorm(x, a_ref, s_ref), w_in_ref[...])
    r = jnp.maximum(a, 0.0)
    r_ref[...] = (r * r).astype(_BF16)
    y = _dot(r_ref[...], w_out_ref[...])
    o_ref[...] = _residual(x, y, g_ref)


def _ffn_sublayer(x, mod, layer, w_in, w_out, seq, next_weights):
    d = x.shape[1]
    tm = FFN_TILE_ROWS
    base = (layer * N_MOD + 3) * MOD_ROWS
    x_spec, a_spec, s_spec, g_spec, out_spec = _tile_specs(tm, d, seq // tm, base)
    return _sublayer_call(
        _ffn_kernel, 6, f"ffn_sublayer_{layer}", x, tm,
        [x_spec, a_spec, s_spec, g_spec, _resident(w_in.shape), _resident(w_out.shape)],
        [x, mod, mod, mod, w_in, w_out],
        out_spec,
        [pltpu.VMEM((tm, FFN_WIDTH), _BF16)],
        next_weights)


def _gelu_tanh(z):
    c = math.sqrt(2.0 / math.pi)
    return 0.5 * z * (1.0 + jnp.tanh(c * (z + 0.044715 * (z * z * z))))


def _sg_kernel(*refs, n_inputs, n_next):
    (x_ref, a_ref, s_ref, g_ref, w_in_ref, ln_g_ref, ln_b_ref, w_s_ref, b_s_ref, w_out_ref,
     o_ref, z_ref, u_ref, vn_ref, y_ref) = _cast_next_weights(refs, n_inputs, n_next)
    tm = x_ref.shape[0]
    x = x_ref[...]
    z_ref[...] = _dot(_prenorm(x, a_ref, s_ref), w_in_ref[...])

    for p in range(tm // SG_PIECE_ROWS):
        rows = slice(p * SG_PIECE_ROWS, (p + 1) * SG_PIECE_ROWS)
        u_ref[rows, :] = _gelu_tanh(z_ref[rows, :SG_WIDTH])
        v = _gelu_tanh(z_ref[rows, SG_WIDTH:])
        mu = jnp.mean(v, axis=-1, keepdims=True)
        vc = v - mu
        var = jnp.mean(vc * vc, axis=-1, keepdims=True)
        vn = vc * lax.rsqrt(var + LN_EPS) * ln_g_ref[...] + ln_b_ref[...]
        vn_ref[rows, :] = vn.astype(_BF16)

    row = lax.broadcasted_iota(jnp.int32, (CHUNK, CHUNK), 0)
    col = lax.broadcasted_iota(jnp.int32, (CHUNK, CHUNK), 1)
    causal = row >= col
    for g in range(SG_GROUPS):
        w_g = jnp.where(causal, w_s_ref[g], 0.0).astype(_BF16)
        b_g = jnp.broadcast_to(b_s_ref[:, g:g + 1], (CHUNK, SG_GROUP_DIM))
        cols = slice(g * SG_GROUP_DIM, (g + 1) * SG_GROUP_DIM)
        for n in range(tm // CHUNK):
            rows = slice(n * CHUNK, (n + 1) * CHUNK)
            mixed = _dot(w_g, vn_ref[rows, cols]) + b_g
            y_ref[rows, cols] = (u_ref[rows, cols] * mixed).astype(_BF16)

    y = _dot(y_ref[...], w_out_ref[...])
    o_ref[...] = _residual(x, y, g_ref)


def _sg_sublayer(x, mod, layer, index, w_in, w_out, ln_g, ln_b, w_s, b_s_t, seq, next_weights):
    d = x.shape[1]
    tm = SG_TILE_ROWS
    base = layer * N_MOD * MOD_ROWS
    x_spec, a_spec, s_spec, g_spec, out_spec = _tile_specs(tm, d, seq // tm, base)
    return _sublayer_call(
        _sg_kernel, 10, f"sg_sublayer_{layer}", x, tm,
        [x_spec, a_spec, s_spec, g_spec,
         _resident(w_in.shape),
         _resident_layer(ln_g.shape, index),
         _resident_layer(ln_b.shape, index),
         _resident_layer(w_s.shape, index),
         _resident_layer(b_s_t.shape, index),
         _resident(w_out.shape)],
        [x, mod, mod, mod, w_in, ln_g, ln_b, w_s, b_s_t, w_out],
        out_spec,
        [
            pltpu.VMEM((tm, 2 * SG_WIDTH), _F32),
            pltpu.VMEM((tm, SG_WIDTH), _F32),
            pltpu.VMEM((tm, SG_WIDTH), _BF16),
            pltpu.VMEM((tm, SG_WIDTH), _BF16),
        ],
        next_weights)


def _rotate(t, cos, sin):
    half = RET_DK // 2
    t1 = t[:, :half]
    t2 = t[:, half:]
    return jnp.concatenate([t1 * cos - t2 * sin, t1 * sin + t2 * cos], axis=-1)


def _retention_head(q, k, v, hd, state_ref, d_intra_ref, d_q_ref, d_k_ref, d_c_ref, ret_ref):
    state = state_ref[hd]
    d_intra = d_intra_ref[hd]
    for first in range(0, q.shape[0] // RET_BLOCK, RET_BLOCK_WINDOW):
        window = range(first, first + RET_BLOCK_WINDOW)
        scores, updates = {}, {}
        for n in window:
            rows = slice(n * RET_BLOCK, (n + 1) * RET_BLOCK)
            scores[n] = lax.dot_general(q[rows].astype(_BF16), k[rows].astype(_BF16),
                                        (((1,), (1,)), ((), ())), preferred_element_type=_F32)
            kd = (k[rows] * d_k_ref[hd]).astype(_BF16)
            updates[n] = lax.dot_general(kd, v[rows], (((0,), (0,)), ((), ())),
                                         preferred_element_type=_F32)
        for n in window:
            rows = slice(n * RET_BLOCK, (n + 1) * RET_BLOCK)
            masked = (scores[n] * d_intra).astype(_BF16)
            qd = (q[rows] * d_q_ref[hd]).astype(_BF16)
            out = _dot(masked, v[rows]) + _dot(qd, state.astype(_BF16))
            ret_ref[rows, hd * RET_DV:(hd + 1) * RET_DV] = out
            state = state * d_c_ref[hd] + updates[n]
    state_ref[hd] = state


def _ret_kernel(*refs, n_inputs, n_next, tiles_per_batch):
    (x_ref, a_ref, s_ref, g_ref, w_in_ref, cos_ref, sin_ref, d_intra_ref, d_q_ref, d_k_ref,
     d_c_ref, gn_g_ref, gn_b_ref, w_out_ref, o_ref, state_ref, ret_ref) = _cast_next_weights(
         refs, n_inputs, n_next)
    @pl.when(pl.program_id(0) % tiles_per_batch == 0)
    def _():
        state_ref[...] = jnp.zeros_like(state_ref)

    x = x_ref[...]
    h = _prenorm(x, a_ref, s_ref)
    cos = cos_ref[...]
    sin = sin_ref[...]

    def project(lo, width):
        return _dot(h, w_in_ref[:, lo:lo + width])

    def project_head(hd):
        q = _rotate(project(hd * RET_DK, RET_DK), cos, sin)
        k = _rotate(project(RET_QK_WIDTH + hd * RET_DK, RET_DK), cos, sin)
        v = project(2 * RET_QK_WIDTH + hd * RET_DV, RET_DV).astype(_BF16)
        return q, k, v

    def retain(hd, qkv):
        _retention_head(*qkv, hd, state_ref, d_intra_ref, d_q_ref, d_k_ref, d_c_ref, ret_ref)

    gate_lo = 2 * RET_QK_WIDTH + RET_V_WIDTH
    half = RET_V_WIDTH // 2

    qkv0 = project_head(0)
    qkv1 = project_head(1)
    retain(0, qkv0)
    qkv2 = project_head(2)
    retain(1, qkv1)
    qkv3 = project_head(3)
    retain(2, qkv2)
    gates = [project(gate_lo, half)]
    retain(3, qkv3)
    gates.append(project(gate_lo + half, half))

    for hd in range(RET_HEADS):
        cols = slice(hd * RET_DV, (hd + 1) * RET_DV)
        o = ret_ref[:, cols]
        mu = jnp.mean(o, axis=-1, keepdims=True)
        oc = o - mu
        var = jnp.mean(oc * oc, axis=-1, keepdims=True)
        on = oc * lax.rsqrt(var + LN_EPS) * gn_g_ref[:, cols] + gn_b_ref[:, cols]
        gate = gates[hd // 2][:, (hd % 2) * RET_DV:(hd % 2 + 1) * RET_DV]
        ret_ref[:, cols] = on * (gate * jax.nn.sigmoid(gate))

    y = _dot(ret_ref[...].astype(_BF16), w_out_ref[...])
    o_ref[...] = _residual(x, y, g_ref)


def _retention_tables(seq):
    half = RET_DK // 2
    inv_freq = ROPE_BASE ** (-jnp.arange(half, dtype=_F32) / half)
    pos = jnp.arange(seq, dtype=_F32)
    ang = pos[:, None] * inv_freq[None, :]
    cos = jnp.cos(ang)
    sin = jnp.sin(ang)
    log_gamma = jnp.log(1.0 - jnp.exp2(-5.0 - jnp.arange(RET_HEADS, dtype=_F32)))
    idx = jnp.arange(RET_BLOCK, dtype=_F32)
    diff = idx[:, None] - idx[None, :]
    d_intra = jnp.where(diff >= 0,
                        jnp.exp(log_gamma[:, None, None] * jnp.maximum(diff, 0.0)), 0.0)
    d_q = jnp.exp(log_gamma[:, None] * (idx + 1.0))
    d_k = jnp.exp(log_gamma[:, None] * (RET_BLOCK - 1.0 - idx))
    d_c = jnp.exp(log_gamma * RET_BLOCK)
    k_scale = RET_DK ** -0.5
    assert math.log2(k_scale).is_integer()
    d_intra = d_intra * k_scale
    d_k = d_k * k_scale
    d_q = jnp.broadcast_to(d_q[:, :, None], (RET_HEADS, RET_BLOCK, RET_DK))
    d_k = jnp.broadcast_to(d_k[:, :, None], (RET_HEADS, RET_BLOCK, RET_DK))
    d_c = jnp.broadcast_to(d_c[:, None, None], (RET_HEADS, 1, RET_DV))
    return cos, sin, d_intra, d_q, d_k, d_c


def _ret_sublayer(x, mod, layer, index, w_in, w_out, gn_g, gn_b, tables, seq, next_weights):
    d = x.shape[1]
    tm = RET_TILE_ROWS
    tpb = seq // tm
    base = layer * N_MOD * MOD_ROWS
    cos, sin, d_intra, d_q, d_k, d_c = tables
    x_spec, a_spec, s_spec, g_spec, out_spec = _tile_specs(tm, d, tpb, base)
    pos_spec = pl.BlockSpec((tm, RET_DK // 2), lambda i: (i % tpb, 0))
    return _sublayer_call(
        functools.partial(_ret_kernel, tiles_per_batch=tpb), 14, f"ret_sublayer_{layer}", x, tm,
        [x_spec, a_spec, s_spec, g_spec,
         _resident(w_in.shape),
         pos_spec,
         pos_spec,
         _resident(d_intra.shape),
         _resident(d_q.shape),
         _resident(d_k.shape),
         _resident(d_c.shape),
         _resident_layer(gn_g.shape, index),
         _resident_layer(gn_b.shape, index),
         _resident(w_out.shape)],
        [x, mod, mod, mod, w_in, cos, sin, d_intra, d_q, d_k, d_c, gn_g, gn_b, w_out],
        out_spec,
        [
            pltpu.VMEM((RET_HEADS, RET_DK, RET_DV), _F32),
            pltpu.VMEM((tm, RET_V_WIDTH), _F32),
        ],
        next_weights)


def kernel(x, c, ada_w, ada_b, pre_mix_g, post_mix_g, pre_ffn_g, post_ffn_g, ffn_w_in, ffn_w_out,
           sg_w_in, sg_ln_g, sg_ln_b, sg_w_s, sg_b_s, sg_w_out,
           ret_w_in, ret_gn_g, ret_gn_b, ret_w_out):
    batch, seq, d = x.shape
    depth = ada_w.shape[0]
    assert d == D_MODEL and batch <= MOD_ROWS
    assert seq % max(FFN_TILE_ROWS, SG_TILE_ROWS, RET_TILE_ROWS) == 0

    ones = jnp.ones_like(pre_mix_g)
    gains = jnp.stack([ones, pre_mix_g, post_mix_g, ones, pre_ffn_g, post_ffn_g], axis=1)
    mod = _ada_modulation(c, ada_w, ada_b, gains.reshape(depth, N_MOD, 1, d))
    mod = mod.reshape(depth * N_MOD * MOD_ROWS, 1, d)

    sg_small = (sg_ln_g[:, None, :], sg_ln_b[:, None, :], sg_w_s, jnp.swapaxes(sg_b_s, 1, 2))
    ret_small = (ret_gn_g[:, None, :], ret_gn_b[:, None, :])
    tables = _retention_tables(seq)

    def mixer_weights(i):
        stacks = (sg_w_in, sg_w_out) if i % 2 == 0 else (ret_w_in, ret_w_out)
        return tuple((w, i // 2) for w in stacks)

    w_in, w_out = (w[j].astype(_BF16) for w, j in mixer_weights(0))
    xf = x.reshape(batch * seq, d)
    for i in range(depth):
        j = i // 2
        ffn_next = ((ffn_w_in, i), (ffn_w_out, i))
        if i % 2 == 0:
            xf, w_in, w_out = _sg_sublayer(xf, mod, i, j, w_in, w_out, *sg_small, seq, ffn_next)
        else:
            xf, w_in, w_out = _ret_sublayer(xf, mod, i, j, w_in, w_out, *ret_small, tables, seq,
                                            ffn_next)
        mixer_next = mixer_weights(i + 1) if i + 1 < depth else ()
        xf, *mixer_w = _ffn_sublayer(xf, mod, i, w_in, w_out, seq, mixer_next)
        if mixer_w:
            w_in, w_out = mixer_w
    return xf.reshape(batch, seq, d)
```
